```python
import math
import jax, jax.numpy as jnp
from jax import lax
import numpy as np

D_MODEL = 2048
BATCH = 2
SEQ = 8192
DEPTH = 4

GRID_W = 64
CTX_LEN = 256
N_EVEN = (DEPTH + 1) // 2
N_ODD = DEPTH // 2
EPS = 1e-6

ATT_HEADS = 8
ATT_KV_HEADS = 2
HEAD_DIM = 128
ATT_WIDTH = ATT_HEADS * HEAD_DIM
KV_WIDTH = ATT_KV_HEADS * HEAD_DIM
CONV_WIDTH = D_MODEL - ATT_WIDTH
SHORT_CONV_K = 3
ROPE_THETA = 10000.0
Q_BLOCK = 128
SPLIT_EVEN = (ATT_WIDTH, ATT_WIDTH + KV_WIDTH, ATT_WIDTH + 2 * KV_WIDTH,
              ATT_WIDTH + 2 * KV_WIDTH + CONV_WIDTH, ATT_WIDTH + 2 * KV_WIDTH + 2 * CONV_WIDTH)
IN_EVEN = ATT_WIDTH + 2 * KV_WIDTH + 3 * CONV_WIDTH

D_INNER = 2 * D_MODEL
SSM_HEAD_DIM = 64
SSM_HEADS = D_INNER // SSM_HEAD_DIM
SSM_GROUPS = 8
HPG = SSM_HEADS // SSM_GROUPS
D_STATE = 128
SSM_CONV_K = 3
CHUNK = 128
CONV_DIM = D_INNER + 4 * SSM_GROUPS * D_STATE
IN_ODD = D_INNER + CONV_DIM + 2 * SSM_HEADS

N_GROUPS = 4
EXP_PER_GROUP = 8
N_EXPERTS = N_GROUPS * EXP_PER_GROUP
TOP_K = 2
D_EXPERT = 768
MOE_BLOCK = 128

kernel_name = "hybrid_diffusion_attnconv_ssd_hmoe"


def rms_norm(x, g):
    xf = x.astype(jnp.float32)
    y = xf * lax.rsqrt(jnp.mean(xf * xf, axis=-1, keepdims=True) + EPS)
    return (y * g.astype(jnp.float32)).astype(x.dtype)


def ada_mod(cond, w, b):
    m = jax.nn.silu(cond) @ w + b
    m = m.reshape(m.shape[0], 6, 1, D_MODEL)
    return [m[:, j] for j in range(6)]


def dwconv_centred(u, w):
    k = w.shape[0]
    p = k // 2
    l = u.shape[1]
    up = jnp.pad(u, ((0, 0), (p, p), (0, 0)))
    return sum(up[:, j:j + l] * w[j] for j in range(k))


def axial_rope_tables(n_tok):
    n_rows = n_tok // GRID_W
    row = jnp.repeat(jnp.arange(n_rows, dtype=jnp.float32), GRID_W)
    col = jnp.tile(jnp.arange(GRID_W, dtype=jnp.float32), n_rows)
    axis_dim = HEAD_DIM // 2
    inv = ROPE_THETA ** (-jnp.arange(0, axis_dim, 2, dtype=jnp.float32) / axis_dim)
    ang = jnp.concatenate([row[:, None] * inv, col[:, None] * inv], axis=-1)
    return jnp.cos(ang), jnp.sin(ang)


def apply_axial_rope(x, cos, sin):
    xf = x.astype(jnp.float32)
    half = HEAD_DIM // 2
    q = half // 2
    parts = []
    for a in range(2):
        xa = xf[..., a * half:(a + 1) * half]
        x1, x2 = xa[..., :q], xa[..., q:]
        ca = cos[:, None, a * q:(a + 1) * q]
        sa = sin[:, None, a * q:(a + 1) * q]
        parts += [x1 * ca - x2 * sa, x1 * sa + x2 * ca]
    return jnp.concatenate(parts, axis=-1).astype(x.dtype)


def softmax_attend(q, k, v):
    s = jnp.einsum('bqkgd,bskd->bkgqs', q.astype(jnp.float32), k) * (HEAD_DIM ** -0.5)
    p = jax.nn.softmax(s, axis=-1)
    return jnp.einsum('bkgqs,bskd->bqkgd', p, v)


def attn_conv_mixer(n_lat, n_ctx, w_in, w_out, q_g, k_g, conv_w, rope_cos, rope_sin, need_ctx):
    grp = ATT_HEADS // ATT_KV_HEADS

    def project(h, rotary):
        b, l, _ = h.shape
        p = h @ w_in
        q, k, v, gb, gc, gx = jnp.split(p, SPLIT_EVEN, axis=-1)
        q = rms_norm(q.reshape(b, l, ATT_HEADS, HEAD_DIM), q_g)
        k = rms_norm(k.reshape(b, l, ATT_KV_HEADS, HEAD_DIM), k_g)
        if rotary:
            q = apply_axial_rope(q, rope_cos, rope_sin)
            k = apply_axial_rope(k, rope_cos, rope_sin)
        v = v.reshape(b, l, ATT_KV_HEADS, HEAD_DIM)
        return q, k, v, gb, gc * gx

    q_l, k_l, v_l, gb_l, u_l = project(n_lat, True)
    q_c, k_c, v_c, gb_c, u_c = project(n_ctx, False)
    b, s, _ = n_lat.shape
    cl = n_ctx.shape[1]
    k_all = jnp.concatenate([k_c, k_l], axis=1).astype(jnp.float32)
    v_all = jnp.concatenate([v_c, v_l], axis=1).astype(jnp.float32)
    qb = jnp.moveaxis(q_l.reshape(b, s // Q_BLOCK, Q_BLOCK, ATT_KV_HEADS, grp, HEAD_DIM), 1, 0)
    o = lax.map(lambda qq: softmax_attend(qq, k_all, v_all), qb)
    o_lat = jnp.moveaxis(o, 0, 1).reshape(b, s, ATT_WIDTH).astype(n_lat.dtype)
    conv_lat = gb_l * dwconv_centred(u_l, conv_w)
    y_lat = jnp.concatenate([o_lat, conv_lat], axis=-1) @ w_out
    y_ctx = None
    if need_ctx:
        o_ctx = softmax_attend(q_c.reshape(b, cl, ATT_KV_HEADS, grp, HEAD_DIM),
                               k_c.astype(jnp.float32), v_c.astype(jnp.float32))
        o_ctx = o_ctx.reshape(b, cl, ATT_WIDTH).astype(n_ctx.dtype)
        conv_ctx = gb_c * dwconv_centred(u_c, conv_w)
        y_ctx = jnp.concatenate([o_ctx, conv_ctx], axis=-1) @ w_out
    return y_lat, y_ctx


def ssd_chunk_scan(x, dt, A, B, C, h0):
    b, l = x.shape[:2]
    nc = l // CHUNK

    def to_chunks(t):
        return jnp.moveaxis(t.reshape((b, nc, CHUNK) + t.shape[2:]), 1, 0)

    xs = to_chunks(x.astype(jnp.float32).reshape(b, l, SSM_GROUPS, HPG, SSM_HEAD_DIM))
    dts = to_chunks(dt.reshape(b, l, SSM_GROUPS, HPG))
    Bs = to_chunks(B.astype(jnp.float32))
    Cs = to_chunks(C.astype(jnp.float32))
    Ag = A.reshape(SSM_GROUPS, HPG)
    mask = jnp.tril(jnp.ones((CHUNK, CHUNK), dtype=bool))[None, :, :, None, None]

    def step(h, inp):
        xc, dtc, Bc, Cc = inp
        a = jnp.cumsum(dtc * Ag, axis=1)
        seg = a[:, :, None] - a[:, None, :]
        decay = jnp.exp(jnp.where(mask, seg, -jnp.inf))
        cb = jnp.einsum('blgn,bsgn->blsg', Cc, Bc)
        xdt = xc * dtc[..., None]
        y_diag = jnp.einsum('blsg,blsgk,bsgkp->blgkp', cb, decay, xdt)
        y_off = jnp.einsum('blgn,bgkpn->blgkp', Cc, h) * jnp.exp(a)[..., None]
        a_last = a[:, -1]
        w_s = jnp.exp(a_last[:, None] - a)
        h_new = h * jnp.exp(a_last)[..., None, None] + jnp.einsum('bsgn,bsgk,bsgkp->bgkpn', Bc, w_s, xdt)
        return h_new, y_diag + y_off

    h_fin, ys = lax.scan(step, h0, (xs, dts, Bs, Cs))
    y = jnp.moveaxis(ys, 0, 1).reshape(b, l, SSM_HEADS, SSM_HEAD_DIM)
    return y, h_fin


def ssd_mixer(n_lat, n_ctx, w_in, conv_w, conv_b, dt_bias, a_log, d_skip, norm_w, w_out, need_ctx):
    A = -jnp.exp(a_log.astype(jnp.float32))

    def prepare(h):
        b, l, _ = h.shape
        p = h @ w_in
        z = p[..., :D_INNER]
        xbc = jax.nn.silu(dwconv_centred(p[..., D_INNER:D_INNER + CONV_DIM], conv_w) + conv_b)
        dt = jax.nn.softplus(p[..., D_INNER + CONV_DIM:].astype(jnp.float32)
                             + dt_bias.reshape(-1).astype(jnp.float32)).reshape(b, l, 2, SSM_HEADS)
        xs = xbc[..., :D_INNER].reshape(b, l, SSM_HEADS, SSM_HEAD_DIM)
        bc = xbc[..., D_INNER:].reshape(b, l, 4, SSM_GROUPS, D_STATE)
        return z, xs, bc, dt

    z_l, x_l, bc_l, dt_l = prepare(n_lat)
    z_c, x_c, bc_c, dt_c = prepare(n_ctx)
    b = n_lat.shape[0]
    h0 = jnp.zeros((b, SSM_GROUPS, HPG, SSM_HEAD_DIM, D_STATE), jnp.float32)

    def rev(t):
        return t[:, ::-1]

    yc_f, hc_f = ssd_chunk_scan(x_c, dt_c[:, :, 0], A[0], bc_c[:, :, 0], bc_c[:, :, 1], h0)
    yl_f, _ = ssd_chunk_scan(x_l, dt_l[:, :, 0], A[0], bc_l[:, :, 0], bc_l[:, :, 1], hc_f)
    yc_b, hc_b = ssd_chunk_scan(rev(x_c), rev(dt_c[:, :, 1]), A[1], rev(bc_c[:, :, 2]), rev(bc_c[:, :, 3]), h0)
    yl_b, _ = ssd_chunk_scan(rev(x_l), rev(dt_l[:, :, 1]), A[1], rev(bc_l[:, :, 2]), rev(bc_l[:, :, 3]), hc_b)

    def finish(y_f, y_b_rev, xs, z):
        bb, l = xs.shape[:2]
        y = y_f + rev(y_b_rev) + d_skip.astype(jnp.float32)[:, None] * xs.astype(jnp.float32)
        y = y.reshape(bb, l, D_INNER) * jax.nn.silu(z.astype(jnp.float32))
        y = y.reshape(bb, l, SSM_GROUPS, D_INNER // SSM_GROUPS)
        y = y * lax.rsqrt(jnp.mean(y * y, axis=-1, keepdims=True) + EPS)
        y = (y.reshape(bb, l, D_INNER) * norm_w.astype(jnp.float32)).astype(z.dtype)
        return y @ w_out

    y_lat = finish(yl_f, yl_b, x_l, z_l)
    y_ctx = finish(yc_f, yc_b, x_c, z_c) if need_ctx else None
    return y_lat, y_ctx


def hier_moe(h, w_grp, b_grp, w_exp, b_exp, w_gate, w_up, w_down):
    T, D = h.shape
    hf = h.astype(jnp.float32)
    g_logits = hf @ w_grp.astype(jnp.float32) + b_grp.astype(jnp.float32)
    g_prob = jax.nn.softmax(g_logits, axis=-1)
    g_p, g_sel = lax.top_k(g_prob, 1)
    e_logits = (hf @ w_exp.astype(jnp.float32) + b_exp.astype(jnp.float32)).reshape(T, N_GROUPS, EXP_PER_GROUP)
    e_in = jnp.take_along_axis(e_logits, g_sel[:, :, None], axis=1)[:, 0]
    e_prob = jax.nn.softmax(e_in, axis=-1)
    top_p, top_i = lax.top_k(e_prob, TOP_K)
    gate = g_p * top_p / jnp.sum(top_p, axis=-1, keepdims=True)
    eid = g_sel * EXP_PER_GROUP + top_i

    A_n = T * TOP_K
    flat_e = eid.reshape(-1)
    flat_t = jnp.repeat(jnp.arange(T, dtype=jnp.int32), TOP_K)
    flat_w = gate.reshape(-1)
    order = jnp.argsort(flat_e)
    se, st, sw = flat_e[order], flat_t[order], flat_w[order]
    counts = jnp.bincount(flat_e, length=N_EXPERTS)
    starts = jnp.cumsum(counts) - counts
    pcounts = (counts + MOE_BLOCK - 1) // MOE_BLOCK * MOE_BLOCK
    pends = jnp.cumsum(pcounts)
    pstarts = pends - pcounts
    dest = pstarts[se] + (jnp.arange(A_n) - starts[se])
    n_blocks = -(-A_n // MOE_BLOCK) + N_EXPERTS
    n_slots = n_blocks * MOE_BLOCK
    slot_t = jnp.zeros((n_slots,), jnp.int32).at[dest].set(st)
    slot_w = jnp.zeros((n_slots,), jnp.float32).at[dest].set(sw)
    blk_e = jnp.minimum(jnp.searchsorted(pends, jnp.arange(n_blocks) * MOE_BLOCK, side='right'), N_EXPERTS - 1)
    xb = h[slot_t].reshape(n_blocks, MOE_BLOCK, D)

    def expert_block(args):
        xblk, e = args
        return (jax.nn.silu(xblk @ w_gate[e]) * (xblk @ w_up[e])) @ w_down[e]

    yb = lax.map(expert_block, (xb, blk_e)).reshape(n_slots, D)
    yb = yb * slot_w[:, None].astype(yb.dtype)
    return jnp.zeros((T, D), h.dtype).at[slot_t].add(yb)


def setup_inputs(seed: int = 0) -> dict:
    key = jax.random.key(seed)
    ks = jax.random.split(key, 32)
    f32 = jnp.float32
    D = D_MODEL

    def nrm(k, shape, scale):
        return jax.random.normal(k, shape, f32) * scale

    dt0 = jnp.exp(jax.random.uniform(ks[17], (N_ODD, 2, SSM_HEADS), f32)
                  * (math.log(0.1) - math.log(0.001)) + math.log(0.001))
    return {
        "x": nrm(ks[0], (BATCH, SEQ, D), 1.0),
        "c": nrm(ks[1], (BATCH, D), 1.0),
        "ctx": nrm(ks[2], (BATCH, CTX_LEN, D), 1.0),
        "c_ctx": nrm(ks[3], (D,), 1.0),
        "ada_w": nrm(ks[4], (DEPTH, D, 6 * D), 0.5 * D ** -0.5),
        "ada_b": nrm(ks[5], (DEPTH, 6 * D), 0.02),
        "norm_mix_g": 1.0 + nrm(ks[6], (DEPTH, D), 0.02),
        "norm_ffn_g": 1.0 + nrm(ks[7], (DEPTH, D), 0.02),
        "att_w_in": nrm(ks[8], (N_EVEN, D, IN_EVEN), D ** -0.5),
        "att_w_out": nrm(ks[9], (N_EVEN, ATT_WIDTH + CONV_WIDTH, D), (ATT_WIDTH + CONV_WIDTH) ** -0.5),
        "att_q_g": 1.0 + nrm(ks[10], (N_EVEN, HEAD_DIM), 0.02),
        "att_k_g": 1.0 + nrm(ks[11], (N_EVEN, HEAD_DIM), 0.02),
        "sconv_w": nrm(ks[12], (N_EVEN, SHORT_CONV_K, CONV_WIDTH), SHORT_CONV_K ** -0.5),
        "ssm_w_in": nrm(ks[13], (N_ODD, D, IN_ODD), D ** -0.5),
        "ssm_conv_w": nrm(ks[14], (N_ODD, SSM_CONV_K, CONV_DIM), SSM_CONV_K ** -0.5),
        "ssm_conv_b": nrm(ks[15], (N_ODD, CONV_DIM), 0.02),
        "ssm_dt_bias": dt0 + jnp.log(-jnp.expm1(-dt0)),
        "ssm_a_log": jnp.log(jax.random.uniform(ks[18], (N_ODD, 2, SSM_HEADS), f32, 1.0, 16.0)),
        "ssm_d": 1.0 + nrm(ks[19], (N_ODD, SSM_HEADS), 0.02),
        "ssm_norm_w": 1.0 + nrm(ks[20], (N_ODD, D_INNER), 0.02),
        "ssm_w_out": nrm(ks[21], (N_ODD, D_INNER, D), D_INNER ** -0.5),
        "router_grp_w": nrm(ks[22], (DEPTH, D, N_GROUPS), D ** -0.5),
        "router_grp_b": nrm(ks[23], (DEPTH, N_GROUPS), 0.01),
        "router_exp_w": nrm(ks[24], (DEPTH, D, N_EXPERTS), D ** -0.5),
        "router_exp_b": nrm(ks[25], (DEPTH, N_EXPERTS), 0.01),
        "exp_w_gate": nrm(ks[26], (DEPTH, N_EXPERTS, D, D_EXPERT), D ** -0.5),
        "exp_w_up": nrm(ks[27], (DEPTH, N_EXPERTS, D, D_EXPERT), D ** -0.5),
        "exp_w_down": nrm(ks[28], (DEPTH, N_EXPERTS, D_EXPERT, D), D_EXPERT ** -0.5),
    }


def reference(x, c, ctx, c_ctx, ada_w, ada_b, norm_mix_g, norm_ffn_g,
              att_w_in, att_w_out, att_q_g, att_k_g, sconv_w,
              ssm_w_in, ssm_conv_w, ssm_conv_b, ssm_dt_bias, ssm_a_log, ssm_d, ssm_norm_w, ssm_w_out,
              router_grp_w, router_grp_b, router_exp_w, router_exp_b,
              exp_w_gate, exp_w_up, exp_w_down):
    b, s, d = x.shape
    cl = ctx.shape[1]
    rope_cos, rope_sin = axial_rope_tables(s)
    h_lat, h_ctx = x, ctx
    for i in range(DEPTH):
        last = i == DEPTH - 1
        sh1_l, sc1_l, g1_l, sh2_l, sc2_l, g2_l = ada_mod(c, ada_w[i], ada_b[i])
        sh1_c, sc1_c, g1_c, sh2_c, sc2_c, g2_c = ada_mod(c_ctx[None], ada_w[i], ada_b[i])
        n_l = rms_norm(h_lat, norm_mix_g[i]) * (1.0 + sc1_l) + sh1_l
        n_c = rms_norm(h_ctx, norm_mix_g[i]) * (1.0 + sc1_c) + sh1_c
        if i % 2 == 0:
            j = i // 2
            y_l, y_c = attn_conv_mixer(n_l, n_c, att_w_in[j], att_w_out[j], att_q_g[j], att_k_g[j],
                                       sconv_w[j], rope_cos, rope_sin, not last)
        else:
            j = i // 2
            y_l, y_c = ssd_mixer(n_l, n_c, ssm_w_in[j], ssm_conv_w[j], ssm_conv_b[j], ssm_dt_bias[j],
                                 ssm_a_log[j], ssm_d[j], ssm_norm_w[j], ssm_w_out[j], not last)
        h_lat = h_lat + g1_l * y_l
        f_l = rms_norm(h_lat, norm_ffn_g[i]) * (1.0 + sc2_l) + sh2_l
        if not last:
            h_ctx = h_ctx + g1_c * y_c
            f_c = rms_norm(h_ctx, norm_ffn_g[i]) * (1.0 + sc2_c) + sh2_c
            tokens = jnp.concatenate([f_l.reshape(-1, d), f_c.reshape(-1, d)], axis=0)
        else:
            tokens = f_l.reshape(-1, d)
        y = hier_moe(tokens, router_grp_w[i], router_grp_b[i], router_exp_w[i], router_exp_b[i],
                     exp_w_gate[i], exp_w_up[i], exp_w_down[i])
        h_lat = h_lat + g2_l * y[:b * s].reshape(b, s, d)
        if not last:
            h_ctx = h_ctx + g2_c * y[b * s:].reshape(b, cl, d)
    return h_lat
```

```python
import functools

import jax
import jax.numpy as jnp
from jax import lax
from jax.experimental import pallas as pl
from jax.experimental.pallas import tpu as pltpu

F32 = jnp.float32
BF16 = jnp.bfloat16
HIGHEST = lax.Precision.HIGHEST

EPS = 1e-6
LANES = 128
SUBLANES = 8
ROW_TILE = 256
GRID_W = 64
ROPE_THETA = 10000.0

ATT_HEADS = 8
ATT_KV_HEADS = 2
ATT_GROUP = ATT_HEADS // ATT_KV_HEADS
HEAD_DIM = 128
ATT_WIDTH = ATT_HEADS * HEAD_DIM
KV_WIDTH = ATT_KV_HEADS * HEAD_DIM
CONV_TILE = 512

SSM_HEAD_DIM = 64
SSM_GROUPS = 8
HPG = 8
D_STATE = 128
CHUNK = 128
GROUP_W = HPG * SSM_HEAD_DIM

N_GROUPS = 4
EXP_PER_GROUP = 8
N_EXPERTS = N_GROUPS * EXP_PER_GROUP
MOE_BLOCK = 256

NEG_INF = float("-inf")


def _cparams(sem, vmem_mb=None):
    kw = dict(dimension_semantics=sem)
    if vmem_mb is not None:
        kw["vmem_limit_bytes"] = vmem_mb * 1024 * 1024
    return pltpu.CompilerParams(**kw)


def _ada_kernel(ct_ref, w_ref, b_ref, o_ref, s_ref, *, n_cond):
    ct = ct_ref[...]
    s_ref[...] = ct * jax.nn.sigmoid(ct)
    d = ct_ref.shape[0]
    tn = o_ref.shape[2]

    def body(i, accs):
        k0 = pl.multiple_of(i * SUBLANES, SUBLANES)
        w = w_ref[0, pl.ds(k0, SUBLANES), :]
        sc = s_ref[pl.ds(k0, SUBLANES), :]
        return tuple(acc + w * sc[:, r:r + 1] for r, acc in enumerate(accs))

    accs = lax.fori_loop(0, d // SUBLANES, body,
                         tuple(jnp.zeros((SUBLANES, tn), F32) for _ in range(n_cond)))
    rows = [jnp.sum(acc, axis=0, keepdims=True) + b_ref[0] for acc in accs]
    rows += [jnp.zeros((1, tn), F32)] * (SUBLANES - n_cond)
    o_ref[0] = jnp.concatenate(rows, axis=0)


def _ada_all(cond_t, ada_w, ada_b, n_cond):
    depth, d, n = ada_w.shape
    tn = 1024
    return pl.pallas_call(
        functools.partial(_ada_kernel, n_cond=n_cond),
        out_shape=jax.ShapeDtypeStruct((depth, SUBLANES, n), F32),
        grid=(depth, n // tn),
        in_specs=[pl.BlockSpec((d, LANES), lambda l, j: (0, 0)),
                  pl.BlockSpec((1, d, tn), lambda l, j: (l, 0, j)),
                  pl.BlockSpec((1, 1, tn), lambda l, j: (l, 0, j))],
        out_specs=pl.BlockSpec((1, SUBLANES, tn), lambda l, j: (l, 0, j)),
        scratch_shapes=[pltpu.VMEM((d, LANES), F32)],
        compiler_params=_cparams(("arbitrary", "arbitrary")),
        name="ada_mod",
    )(cond_t, ada_w, ada_b.reshape(depth, 1, n))


def _norm_mod_kernel(h_ref, g_ref, m_ref, o_ref, *, shift_row, scale_row):
    x = h_ref[...]
    y = x * lax.rsqrt(jnp.mean(x * x, axis=-1, keepdims=True) + EPS) * g_ref[...]
    sh = m_ref[0, shift_row:shift_row + 1, :]
    sc = m_ref[0, scale_row:scale_row + 1, :]
    o_ref[...] = (y * (1.0 + sc) + sh).astype(o_ref.dtype)


def _cond_of_tile(i, tiles_per_batch, batch):
    return jnp.where(i % tiles_per_batch == 0, batch, i // tiles_per_batch)


def _norm_mod(h, g, modt, shift_row, scale_row, out_dtype, tpb, batch):
    m, d = h.shape
    return pl.pallas_call(
        functools.partial(_norm_mod_kernel, shift_row=shift_row, scale_row=scale_row),
        out_shape=jax.ShapeDtypeStruct((m, d), out_dtype),
        grid=(m // ROW_TILE,),
        in_specs=[pl.BlockSpec((ROW_TILE, d), lambda i: (i, 0)),
                  pl.BlockSpec((1, d), lambda i: (0, 0)),
                  pl.BlockSpec((1, SUBLANES, d), lambda i: (_cond_of_tile(i, tpb, batch), 0, 0))],
        out_specs=pl.BlockSpec((ROW_TILE, d), lambda i: (i, 0)),
        compiler_params=_cparams(("arbitrary",)),
        name="norm_mod",
    )(h, g.reshape(1, d), modt)


def _mm_kernel(x_ref, w_ref, o_ref):
    o_ref[...] = jnp.dot(x_ref[...], w_ref[...], preferred_element_type=F32).astype(o_ref.dtype)


def _mm(x, w, out_dtype, tm, tn):
    m, k = x.shape
    n = w.shape[1]
    return pl.pallas_call(
        _mm_kernel,
        out_shape=jax.ShapeDtypeStruct((m, n), out_dtype),
        grid=(n // tn, m // tm),
        in_specs=[pl.BlockSpec((tm, k), lambda j, i: (i, 0)),
                  pl.BlockSpec((k, tn), lambda j, i: (0, j))],
        out_specs=pl.BlockSpec((tm, tn), lambda j, i: (i, j)),
        compiler_params=_cparams(("arbitrary", "arbitrary")),
        name="mm",
    )(x, w)


def _mm_res_kernel(x_ref, w_ref, r_ref, m_ref, o_ref, *, gate_row):
    acc = jnp.dot(x_ref[...], w_ref[...], preferred_element_type=F32)
    o_ref[...] = r_ref[...] + m_ref[0, gate_row:gate_row + 1, :] * acc


def _mm_res(x, w, res, modt, gate_row, tn, tpb, batch):
    m, k = x.shape
    n = w.shape[1]
    return pl.pallas_call(
        functools.partial(_mm_res_kernel, gate_row=gate_row),
        out_shape=jax.ShapeDtypeStruct((m, n), F32),
        grid=(n // tn, m // ROW_TILE),
        in_specs=[pl.BlockSpec((ROW_TILE, k), lambda j, i: (i, 0)),
                  pl.BlockSpec((k, tn), lambda j, i: (0, j)),
                  pl.BlockSpec((ROW_TILE, tn), lambda j, i: (i, j)),
                  pl.BlockSpec((1, SUBLANES, tn), lambda j, i: (_cond_of_tile(i, tpb, batch), 0, j))],
        out_specs=pl.BlockSpec((ROW_TILE, tn), lambda j, i: (i, j)),
        compiler_params=_cparams(("arbitrary", "arbitrary")),
        name="mm_res",
    )(x, w, res, modt)


def _qkv_prep_kernel(p_ref, cos_ref, sin_ref, qg_ref, kg_ref, q_ref, k_ref, v_ref):
    cos = cos_ref[...]
    sin = sin_ref[...]
    lane = lax.broadcasted_iota(jnp.int32, cos.shape, 1)
    first_half = (lane % 64) < 32

    def norm_rope(x, g):
        y = x * lax.rsqrt(jnp.mean(x * x, axis=-1, keepdims=True) + EPS) * g
        partner = jnp.where(first_half, pltpu.roll(y, 96, axis=1), pltpu.roll(y, 32, axis=1))
        return y * cos + partner * sin

    scale = HEAD_DIM ** -0.5
    for h in range(ATT_HEADS):
        x = p_ref[:, h * HEAD_DIM:(h + 1) * HEAD_DIM]
        q_ref[:, h * HEAD_DIM:(h + 1) * HEAD_DIM] = (norm_rope(x, qg_ref[...]) * scale).astype(q_ref.dtype)
    for h in range(ATT_KV_HEADS):
        c0 = ATT_WIDTH + h * HEAD_DIM
        k_ref[:, h * HEAD_DIM:(h + 1) * HEAD_DIM] = norm_rope(p_ref[:, c0:c0 + HEAD_DIM], kg_ref[...]).astype(k_ref.dtype)
    v_ref[...] = p_ref[:, ATT_WIDTH + KV_WIDTH:ATT_WIDTH + 2 * KV_WIDTH].astype(v_ref.dtype)


def _qkv_prep(p, cos_t, sin_t, q_g, k_g, tiles_per_batch):
    m = p.shape[0]
    w = ATT_WIDTH + 2 * KV_WIDTH
    return pl.pallas_call(
        _qkv_prep_kernel,
        out_shape=(jax.ShapeDtypeStruct((m, ATT_WIDTH), BF16),
                   jax.ShapeDtypeStruct((m, KV_WIDTH), BF16),
                   jax.ShapeDtypeStruct((m, KV_WIDTH), BF16)),
        grid=(m // ROW_TILE,),
        in_specs=[pl.BlockSpec((ROW_TILE, w), lambda i: (i, 0)),
                  pl.BlockSpec((ROW_TILE, HEAD_DIM), lambda i: (i % tiles_per_batch, 0)),
                  pl.BlockSpec((ROW_TILE, HEAD_DIM), lambda i: (i % tiles_per_batch, 0)),
                  pl.BlockSpec((1, HEAD_DIM), lambda i: (0, 0)),
                  pl.BlockSpec((1, HEAD_DIM), lambda i: (0, 0))],
        out_specs=(pl.BlockSpec((ROW_TILE, ATT_WIDTH), lambda i: (i, 0)),
                   pl.BlockSpec((ROW_TILE, KV_WIDTH), lambda i: (i, 0)),
                   pl.BlockSpec((ROW_TILE, KV_WIDTH), lambda i: (i, 0))),
        compiler_params=_cparams(("arbitrary",)),
        name="qkv_prep",
    )(p, cos_t, sin_t, q_g.reshape(1, HEAD_DIM), k_g.reshape(1, HEAD_DIM))


def _flash_kernel(q_ref, k_ref, v_ref, o_ref, qs_ref, m_ref, l_ref, acc_ref, *, ctx_len):
    qi = pl.program_id(2)
    ki = pl.program_id(3)
    tq = q_ref.shape[1]
    tk = k_ref.shape[1]

    @pl.when(ki == 0)
    def _():
        for g in range(ATT_GROUP):
            qs_ref[g * tq:(g + 1) * tq, :] = q_ref[0, :, g * HEAD_DIM:(g + 1) * HEAD_DIM]
        m_ref[...] = jnp.full(m_ref.shape, NEG_INF, F32)
        l_ref[...] = jnp.zeros(l_ref.shape, F32)
        acc_ref[...] = jnp.zeros(acc_ref.shape, F32)

    def step(ctx_only):
        s = lax.dot_general(qs_ref[...], k_ref[0], (((1,), (1,)), ((), ())), preferred_element_type=F32)
        if ctx_only:
            col = lax.broadcasted_iota(jnp.int32, s.shape, 1)
            s = jnp.where(col < ctx_len, s, NEG_INF)
        m_prev = m_ref[...]
        m_new = jnp.maximum(m_prev, jnp.max(s, axis=-1, keepdims=True))
        alpha = jnp.exp(m_prev - m_new)
        p = jnp.exp(s - m_new)
        l_ref[...] = alpha * l_ref[...] + jnp.sum(p, axis=-1, keepdims=True)
        acc_ref[...] = alpha * acc_ref[...] + jnp.dot(p.astype(BF16), v_ref[0], preferred_element_type=F32)
        m_ref[...] = m_new

    pl.when(jnp.logical_and(qi == 0, ki == 0))(functools.partial(step, True))
    pl.when(qi > 0)(functools.partial(step, False))

    @pl.when(ki == pl.num_programs(3) - 1)
    def _():
        out = acc_ref[...] / l_ref[...]
        for g in range(ATT_GROUP):
            o_ref[0, :, g * HEAD_DIM:(g + 1) * HEAD_DIM] = out[g * tq:(g + 1) * tq, :].astype(o_ref.dtype)


def _flash(q, k, v, ctx_len, tk):
    b, l, _ = q.shape
    tq = ROW_TILE
    assert ctx_len == tq and tk >= ctx_len and l % tk == 0
    gw = ATT_GROUP * HEAD_DIM
    kv_idx = lambda bi, h, qi, ki: (bi, jnp.where(qi == 0, 0, ki), h)
    return pl.pallas_call(
        functools.partial(_flash_kernel, ctx_len=ctx_len),
        out_shape=jax.ShapeDtypeStruct((b, l, ATT_WIDTH), BF16),
        grid=(b, ATT_KV_HEADS, l // tq, l // tk),
        in_specs=[pl.BlockSpec((1, tq, gw), lambda bi, h, qi, ki: (bi, qi, h)),
                  pl.BlockSpec((1, tk, HEAD_DIM), kv_idx),
                  pl.BlockSpec((1, tk, HEAD_DIM), kv_idx)],
        out_specs=pl.BlockSpec((1, tq, gw), lambda bi, h, qi, ki: (bi, qi, h)),
        scratch_shapes=[pltpu.VMEM((ATT_GROUP * tq, HEAD_DIM), BF16),
                        pltpu.VMEM((ATT_GROUP * tq, 1), F32),
                        pltpu.VMEM((ATT_GROUP * tq, 1), F32),
                        pltpu.VMEM((ATT_GROUP * tq, HEAD_DIM), F32)],
        compiler_params=_cparams(("arbitrary", "arbitrary", "arbitrary", "arbitrary")),
        name="flash",
    )(q, k, v)


def _conv3(u, prev_row, next_row, w_ref):
    rows = u.shape[0]
    row = lax.broadcasted_iota(jnp.int32, u.shape, 0)
    u_prev = jnp.where(row == 0, prev_row, pltpu.roll(u, 1, axis=0))
    u_next = jnp.where(row == rows - 1, next_row, pltpu.roll(u, rows - 1, axis=0))
    return u_prev * w_ref[0:1, :] + u * w_ref[1:2, :] + u_next * w_ref[2:3, :]


def _seq_edges(tiles_per_batch):
    tb = pl.program_id(0) % tiles_per_batch
    has_prev = tb >= 2
    has_next = jnp.logical_and(tb >= 1, tb < tiles_per_batch - 1)
    return has_prev, has_next


def _halo_specs(col_block, n_rows):
    rb = ROW_TILE // SUBLANES
    last = n_rows // SUBLANES - 1
    prev = pl.BlockSpec((SUBLANES, CONV_TILE), lambda i, j: (jnp.maximum(i * rb - 1, 0), col_block + j))
    nxt = pl.BlockSpec((SUBLANES, CONV_TILE), lambda i, j: (jnp.minimum((i + 1) * rb, last), col_block + j))
    return prev, nxt


def _sconv_kernel(gb_ref, gc_ref, gx_ref, gcp_ref, gxp_ref, gcn_ref, gxn_ref, w_ref, o_ref, *, tiles_per_batch):
    has_prev, has_next = _seq_edges(tiles_per_batch)
    u = gc_ref[...] * gx_ref[...]
    prev_row = jnp.where(has_prev, gcp_ref[SUBLANES - 1:SUBLANES, :] * gxp_ref[SUBLANES - 1:SUBLANES, :], 0.0)
    next_row = jnp.where(has_next, gcn_ref[0:1, :] * gxn_ref[0:1, :], 0.0)
    o_ref[...] = (gb_ref[...] * _conv3(u, prev_row, next_row, w_ref)).astype(o_ref.dtype)


def _sconv(p, conv_w, tiles_per_batch):
    m = p.shape[0]
    width = conv_w.shape[1]
    base = (ATT_WIDTH + 2 * KV_WIDTH) // CONV_TILE
    nb = width // CONV_TILE
    main = lambda off: pl.BlockSpec((ROW_TILE, CONV_TILE), lambda i, j: (i, off + j))
    gcp, gcn = _halo_specs(base + nb, m)
    gxp, gxn = _halo_specs(base + 2 * nb, m)
    return pl.pallas_call(
        functools.partial(_sconv_kernel, tiles_per_batch=tiles_per_batch),
        out_shape=jax.ShapeDtypeStruct((m, width), BF16),
        grid=(m // ROW_TILE, nb),
        in_specs=[main(base), main(base + nb), main(base + 2 * nb), gcp, gxp, gcn, gxn,
                  pl.BlockSpec((3, CONV_TILE), lambda i, j: (0, j))],
        out_specs=pl.BlockSpec((ROW_TILE, CONV_TILE), lambda i, j: (i, j)),
        compiler_params=_cparams(("arbitrary", "arbitrary")),
        name="sconv",
    )(p, p, p, p, p, p, p, conv_w)


def _ssd_prep_kernel(u_ref, up_ref, un_ref, w_ref, b_ref, o_ref, *, tiles_per_batch):
    has_prev, has_next = _seq_edges(tiles_per_batch)
    prev_row = jnp.where(has_prev, up_ref[SUBLANES - 1:SUBLANES, :], 0.0)
    next_row = jnp.where(has_next, un_ref[0:1, :], 0.0)
    y = _conv3(u_ref[...], prev_row, next_row, w_ref) + b_ref[...]
    o_ref[0] = (y * jax.nn.sigmoid(y)).astype(o_ref.dtype)


def _ssd_prep(p, conv_w, conv_b, col_off, tiles_per_batch):
    m = p.shape[0]
    width = conv_w.shape[1]
    base = col_off // CONV_TILE
    nb = width // CONV_TILE
    prev, nxt = _halo_specs(base, m)
    return pl.pallas_call(
        functools.partial(_ssd_prep_kernel, tiles_per_batch=tiles_per_batch),
        out_shape=jax.ShapeDtypeStruct((nb, m, CONV_TILE), F32),
        grid=(m // ROW_TILE, nb),
        in_specs=[pl.BlockSpec((ROW_TILE, CONV_TILE), lambda i, j: (i, base + j)), prev, nxt,
                  pl.BlockSpec((3, CONV_TILE), lambda i, j: (0, j)),
                  pl.BlockSpec((1, CONV_TILE), lambda i, j: (0, j))],
        out_specs=pl.BlockSpec((1, ROW_TILE, CONV_TILE), lambda i, j: (j, i, 0)),
        compiler_params=_cparams(("arbitrary", "arbitrary")),
        name="ssd_prep",
    )(p, p, p, conv_w, conv_b.reshape(1, width))


def _softplus(x):
    return jnp.maximum(x, 0.0) + jnp.log1p(jnp.exp(-jnp.abs(x)))


def _ssd_scan_kernel(x_ref, b_ref, c_ref, dt_ref, dtt_ref, bias_r_ref, bias_c_ref, alog_r_ref, alog_c_ref,
                     y_ref, h_ref, xw_ref, sc_ref, *, direction):
    @pl.when(pl.program_id(1) == 0)
    def _():
        h_ref[...] = jnp.zeros(h_ref.shape, F32)

    dt = _softplus(dt_ref[...] + bias_r_ref[...])
    dtt = _softplus(dtt_ref[...] + bias_c_ref[...])
    dta = dt * (-jnp.exp(alog_r_ref[...]))
    dtat = dtt * (-jnp.exp(alog_c_ref[...]))
    row = lax.broadcasted_iota(jnp.int32, (CHUNK, CHUNK), 0)
    col = lax.broadcasted_iota(jnp.int32, (CHUNK, CHUNK), 1)
    mask = (row >= col) if direction == 0 else (row <= col)
    tri = mask.astype(F32)
    a_all = jnp.dot(tri, dta, precision=HIGHEST, preferred_element_type=F32)
    at_all = lax.dot_general(dtat, tri, (((1,), (1,)), ((), ())), precision=HIGHEST,
                             preferred_element_type=F32)
    a_tot = jnp.sum(dta, axis=0, keepdims=True)

    for g in range(SSM_GROUPS):
        sl = slice((g % 4) * D_STATE, (g % 4 + 1) * D_STATE)
        bg = b_ref[g // 4, :, sl]
        cg = c_ref[g // 4, :, sl].astype(BF16)
        cb = lax.dot_general(cg, bg.astype(BF16), (((1,), (1,)), ((), ())), preferred_element_type=F32)
        y_off = jnp.dot(cg, h_ref[g].astype(BF16), preferred_element_type=F32)
        for k in range(HPG):
            hd = direction * SSM_GROUPS * HPG + g * HPG + k
            ch = slice(k * SSM_HEAD_DIM, (k + 1) * SSM_HEAD_DIM)
            a_col = a_all[:, hd:hd + 1]
            a_row = at_all[hd:hd + 1, :]
            decay = jnp.exp(jnp.where(mask, a_col - a_row, NEG_INF))
            xdt = x_ref[g, :, ch] * dt[:, hd:hd + 1]
            y_diag = jnp.dot((cb * decay).astype(BF16), xdt.astype(BF16), preferred_element_type=F32)
            y_ref[g, :, ch] = y_diag + y_off[:, ch] * jnp.exp(a_col)
            a_last = a_tot[:, hd:hd + 1]
            xw_ref[:, ch] = (xdt * jnp.exp(a_last - a_col)).astype(BF16)
            sc_ref[:, ch] = jnp.broadcast_to(jnp.exp(a_last), (1, SSM_HEAD_DIM))
        h_ref[g] = h_ref[g] * sc_ref[...] + jnp.dot(bg.T.astype(BF16), xw_ref[...], preferred_element_type=F32)


def _ssd_scan(xbc_g, dt_raw, dtt_raw, dt_bias, a_log, direction, batch):
    ng2, m, _ = xbc_g.shape
    chunks = m // CHUNK // batch
    ctx_chunks = ROW_TILE // CHUNK

    def rb(bi, t):
        if direction == 0:
            c = t
        else:
            c = jnp.where(t < ctx_chunks, ctx_chunks - 1 - t, chunks - 1 + ctx_chunks - t)
        return bi * chunks + c

    nh = dt_raw.shape[1]
    row = lambda v: v.reshape(1, nh)
    colv = lambda v: v.reshape(nh, 1)
    b_blk = 4 + 2 * direction
    return pl.pallas_call(
        functools.partial(_ssd_scan_kernel, direction=direction),
        out_shape=jax.ShapeDtypeStruct((SSM_GROUPS, m, GROUP_W), F32),
        grid=(batch, chunks),
        in_specs=[pl.BlockSpec((SSM_GROUPS, CHUNK, GROUP_W), lambda bi, t: (0, rb(bi, t), 0)),
                  pl.BlockSpec((2, CHUNK, GROUP_W), lambda bi, t: (b_blk, rb(bi, t), 0)),
                  pl.BlockSpec((2, CHUNK, GROUP_W), lambda bi, t: (b_blk + 1, rb(bi, t), 0)),
                  pl.BlockSpec((CHUNK, nh), lambda bi, t: (rb(bi, t), 0)),
                  pl.BlockSpec((nh, CHUNK), lambda bi, t: (0, rb(bi, t))),
                  pl.BlockSpec((1, nh), lambda bi, t: (0, 0)),
                  pl.BlockSpec((nh, 1), lambda bi, t: (0, 0)),
                  pl.BlockSpec((1, nh), lambda bi, t: (0, 0)),
                  pl.BlockSpec((nh, 1), lambda bi, t: (0, 0))],
        out_specs=pl.BlockSpec((SSM_GROUPS, CHUNK, GROUP_W), lambda bi, t: (0, rb(bi, t), 0)),
        scratch_shapes=[pltpu.VMEM((SSM_GROUPS, D_STATE, GROUP_W), F32),
                        pltpu.VMEM((CHUNK, GROUP_W), BF16),
                        pltpu.VMEM((1, GROUP_W), F32)],
        compiler_params=_cparams(("arbitrary", "arbitrary")),
        name=f"ssd_scan{direction}",
    )(xbc_g, xbc_g, xbc_g, dt_raw, dtt_raw, row(dt_bias), colv(dt_bias), row(a_log), colv(a_log))


def _ssd_finish_kernel(yf_ref, yb_ref, x_ref, z_ref, d_ref, nw_ref, o_ref):
    y = yf_ref[0] + yb_ref[0] + d_ref[...] * x_ref[0]
    z = z_ref[...]
    y = y * (z * jax.nn.sigmoid(z))
    y = y * lax.rsqrt(jnp.mean(y * y, axis=-1, keepdims=True) + EPS)
    o_ref[...] = (y * nw_ref[...]).astype(o_ref.dtype)


def _ssd_finish(yf, yb, xbc_g, p, d_exp, norm_w):
    _, m, _ = yf.shape
    d_inner = SSM_GROUPS * GROUP_W
    gspec = pl.BlockSpec((1, ROW_TILE, GROUP_W), lambda i, g: (g, i, 0))
    vspec = pl.BlockSpec((1, GROUP_W), lambda i, g: (0, g))
    return pl.pallas_call(
        _ssd_finish_kernel,
        out_shape=jax.ShapeDtypeStruct((m, d_inner), BF16),
        grid=(m // ROW_TILE, SSM_GROUPS),
        in_specs=[gspec, gspec, gspec, pl.BlockSpec((ROW_TILE, GROUP_W), lambda i, g: (i, g)), vspec, vspec],
        out_specs=pl.BlockSpec((ROW_TILE, GROUP_W), lambda i, g: (i, g)),
        compiler_params=_cparams(("arbitrary", "arbitrary")),
        name="ssd_finish",
    )(yf, yb, xbc_g, p, d_exp.reshape(1, d_inner), norm_w.reshape(1, d_inner))


def _router_kernel(f_ref, w_ref, b_ref, o_ref, cnt_ref, carry_ref):
    @pl.when(pl.program_id(0) == 0)
    def _():
        carry_ref[...] = jnp.zeros(carry_ref.shape, F32)

    logits = jnp.dot(f_ref[...], w_ref[...], precision=HIGHEST, preferred_element_type=F32) + b_ref[...]
    rows = logits.shape[0]
    lane = lax.broadcasted_iota(jnp.int32, logits.shape, 1)
    big = jnp.int32(1 << 20)

    def first_lane(cond):
        return jnp.min(jnp.where(cond, lane, big), axis=-1, keepdims=True)

    gmask = lane < N_GROUPS
    gl = jnp.where(gmask, logits, NEG_INF)
    ge = jnp.exp(gl - jnp.max(gl, axis=-1, keepdims=True))
    g_prob = ge / jnp.sum(ge, axis=-1, keepdims=True)
    g_p = jnp.max(g_prob, axis=-1, keepdims=True)
    g_sel = first_lane(jnp.logical_and(gmask, g_prob == g_p))
    lo = N_GROUPS + EXP_PER_GROUP * g_sel
    emask = jnp.logical_and(lane >= lo, lane < lo + EXP_PER_GROUP)
    el = jnp.where(emask, logits, NEG_INF)
    ee = jnp.exp(el - jnp.max(el, axis=-1, keepdims=True))
    e_prob = jnp.where(emask, ee / jnp.sum(ee, axis=-1, keepdims=True), -1.0)
    p1 = jnp.max(e_prob, axis=-1, keepdims=True)
    i1 = first_lane(e_prob == p1)
    e_rest = jnp.where(lane == i1, -1.0, e_prob)
    p2 = jnp.max(e_rest, axis=-1, keepdims=True)
    i2 = first_lane(e_rest == p2)
    denom = p1 + p2
    gate1 = g_p * p1 / denom
    gate2 = g_p * p2 / denom
    e1 = i1 - N_GROUPS
    e2 = i2 - N_GROUPS

    oh1 = lane == e1
    oh2 = lane == e2
    ohs = jnp.logical_or(oh1, oh2).astype(F32)
    r_i = lax.broadcasted_iota(jnp.int32, (rows, rows), 0)
    c_i = lax.broadcasted_iota(jnp.int32, (rows, rows), 1)
    before = (r_i > c_i).astype(F32).astype(BF16)
    seen = jnp.dot(before, ohs.astype(BF16), preferred_element_type=F32) + carry_ref[...]
    rank1 = jnp.sum(jnp.where(oh1, seen, 0.0), axis=-1, keepdims=True)
    rank2 = jnp.sum(jnp.where(oh2, seen, 0.0), axis=-1, keepdims=True)
    carry_ref[...] = carry_ref[...] + jnp.sum(ohs, axis=0, keepdims=True)

    slab = jnp.zeros(logits.shape, F32)
    for idx, val in enumerate((e1.astype(F32), e2.astype(F32), gate1, gate2, rank1, rank2)):
        slab = jnp.where(lane == idx, val, slab)
    o_ref[...] = slab
    cnt_ref[...] = jnp.broadcast_to(carry_ref[...], cnt_ref.shape)


def _router(f, w, b):
    t, d = f.shape
    return pl.pallas_call(
        _router_kernel,
        out_shape=(jax.ShapeDtypeStruct((t, LANES), F32), jax.ShapeDtypeStruct((SUBLANES, LANES), F32)),
        grid=(t // ROW_TILE,),
        in_specs=[pl.BlockSpec((ROW_TILE, d), lambda i: (i, 0)),
                  pl.BlockSpec((d, LANES), lambda i: (0, 0)),
                  pl.BlockSpec((1, LANES), lambda i: (0, 0))],
        out_specs=(pl.BlockSpec((ROW_TILE, LANES), lambda i: (i, 0)),
                   pl.BlockSpec((SUBLANES, LANES), lambda i: (0, 0))),
        scratch_shapes=[pltpu.VMEM((1, LANES), F32)],
        compiler_params=_cparams(("arbitrary",)),
        name="router",
    )(f, w, b)


def _gather_rows(idx_ref, src_hbm, dst_ref, sem, n_rows):
    def body(r, carry):
        t = idx_ref[0, 0, r]
        pltpu.make_async_copy(src_hbm.at[pl.ds(t, 1)], dst_ref.at[pl.ds(r, 1)], sem).start()
        return carry

    lax.fori_loop(0, n_rows, body, 0)


def _wait_rows(src_hbm, dst_ref, sem, n_rows):
    pltpu.make_async_copy(src_hbm.at[pl.ds(0, n_rows)], dst_ref, sem).wait()


def _expert_kernel(blk_e_ref, n_used_ref, cur_ref, nxt_ref, f_hbm, wg_ref, wu_ref, wd_ref, o_ref, xbuf, sems):
    i = pl.program_id(0)
    n_used = n_used_ref[0]
    slot = i % 2

    @pl.when(i == 0)
    def _():
        _gather_rows(cur_ref, f_hbm, xbuf.at[0], sems.at[0], MOE_BLOCK)

    @pl.when(i + 1 < n_used)
    def _():
        _gather_rows(nxt_ref, f_hbm, xbuf.at[1 - slot], sems.at[1 - slot], MOE_BLOCK)

    @pl.when(i < n_used)
    def _():
        _wait_rows(f_hbm, xbuf.at[slot], sems.at[slot], MOE_BLOCK)
        x = xbuf[slot].astype(BF16)
        hg = jnp.dot(x, wg_ref[0], preferred_element_type=F32)
        hu = jnp.dot(x, wu_ref[0], preferred_element_type=F32)
        act = (hg * jax.nn.sigmoid(hg) * hu).astype(BF16)
        o_ref[...] = jnp.dot(act, wd_ref[0], preferred_element_type=F32)

    @pl.when(i >= n_used)
    def _():
        o_ref[...] = jnp.zeros(o_ref.shape, F32)


def _experts(f, slot_t, blk_e, n_used, wg, wu, wd):
    t, d = f.shape
    n_blocks = blk_e.shape[0]
    de = wg.shape[2]
    slots3 = slot_t.reshape(n_blocks, 1, MOE_BLOCK)
    grid_spec = pltpu.PrefetchScalarGridSpec(
        num_scalar_prefetch=2,
        grid=(n_blocks,),
        in_specs=[pl.BlockSpec((1, 1, MOE_BLOCK), lambda i, be, nu: (i, 0, 0), memory_space=pltpu.SMEM),
                  pl.BlockSpec((1, 1, MOE_BLOCK), lambda i, be, nu: (jnp.minimum(i + 1, n_blocks - 1), 0, 0),
                               memory_space=pltpu.SMEM),
                  pl.BlockSpec(memory_space=pl.ANY),
                  pl.BlockSpec((1, d, de), lambda i, be, nu: (be[i], 0, 0)),
                  pl.BlockSpec((1, d, de), lambda i, be, nu: (be[i], 0, 0)),
                  pl.BlockSpec((1, de, d), lambda i, be, nu: (be[i], 0, 0))],
        out_specs=pl.BlockSpec((MOE_BLOCK, d), lambda i, be, nu: (i, 0)),
        scratch_shapes=[pltpu.VMEM((2, MOE_BLOCK, d), F32), pltpu.SemaphoreType.DMA((2,))],
    )
    return pl.pallas_call(
        _expert_kernel,
        out_shape=jax.ShapeDtypeStruct((n_blocks * MOE_BLOCK, d), F32),
        grid_spec=grid_spec,
        compiler_params=_cparams(("arbitrary",)),
        name="experts",
    )(blk_e, n_used, slots3, slots3, f, wg, wu, wd)


def _combine_kernel(cur_ref, nxt_ref, y_hbm, h_ref, slab_ref, m_ref, o_ref, ybuf, sems, *, gate_row):
    i = pl.program_id(0)
    n = pl.num_programs(0)
    slot = i % 2

    def gather(idx_ref, s):
        for k in range(2):
            def body(r, carry, k=k):
                t = idx_ref[0, k, r]
                pltpu.make_async_copy(y_hbm.at[pl.ds(t, 1)], ybuf.at[s, k, pl.ds(r, 1)], sems.at[s, k]).start()
                return carry
            lax.fori_loop(0, ROW_TILE, body, 0)

    @pl.when(i == 0)
    def _():
        gather(cur_ref, 0)

    @pl.when(i + 1 < n)
    def _():
        gather(nxt_ref, 1 - slot)

    for k in range(2):
        pltpu.make_async_copy(y_hbm.at[pl.ds(0, ROW_TILE)], ybuf.at[slot, k], sems.at[slot, k]).wait()
    slab = slab_ref[...]
    y = slab[:, 2:3] * ybuf[slot, 0] + slab[:, 3:4] * ybuf[slot, 1]
    o_ref[...] = h_ref[...] + m_ref[0, gate_row:gate_row + 1, :] * y


def _combine(y_sorted, dest3, h, slab, modt, gate_row, tpb, batch):
    t, d = h.shape
    nt = t // ROW_TILE
    return pl.pallas_call(
        functools.partial(_combine_kernel, gate_row=gate_row),
        out_shape=jax.ShapeDtypeStruct((t, d), F32),
        grid=(nt,),
        in_specs=[pl.BlockSpec((1, 2, ROW_TILE), lambda i: (i, 0, 0), memory_space=pltpu.SMEM),
                  pl.BlockSpec((1, 2, ROW_TILE), lambda i: (jnp.minimum(i + 1, nt - 1), 0, 0),
                               memory_space=pltpu.SMEM),
                  pl.BlockSpec(memory_space=pl.ANY),
                  pl.BlockSpec((ROW_TILE, d), lambda i: (i, 0)),
                  pl.BlockSpec((ROW_TILE, LANES), lambda i: (i, 0)),
                  pl.BlockSpec((1, SUBLANES, d), lambda i: (_cond_of_tile(i, tpb, batch), 0, 0))],
        out_specs=pl.BlockSpec((ROW_TILE, d), lambda i: (i, 0)),
        scratch_shapes=[pltpu.VMEM((2, 2, ROW_TILE, d), F32), pltpu.SemaphoreType.DMA((2, 2))],
        compiler_params=_cparams(("arbitrary",)),
        name="moe_combine",
    )(dest3, dest3, y_sorted, h, slab, modt)


def _moe(f, h, modt, w_grp, b_grp, w_exp, b_exp, wg, wu, wd, tpb, batch):
    t, d = f.shape
    n_r = N_GROUPS + N_EXPERTS
    w_r = jnp.zeros((d, LANES), F32).at[:, :n_r].set(jnp.concatenate([w_grp, w_exp], axis=1))
    b_r = jnp.zeros((1, LANES), F32).at[0, :n_r].set(jnp.concatenate([b_grp, b_exp]))
    slab, cnt = _router(f, w_r, b_r)

    counts = cnt[0, :N_EXPERTS].astype(jnp.int32)
    pcounts = (counts + MOE_BLOCK - 1) // MOE_BLOCK * MOE_BLOCK
    pends = jnp.cumsum(pcounts)
    pstarts = pends - pcounts
    eid = slab[:, 0:2].astype(jnp.int32)
    rank = slab[:, 4:6].astype(jnp.int32)
    dest = pstarts[eid] + rank
    n_blocks = -(-2 * t // MOE_BLOCK) + N_EXPERTS
    n_slots = n_blocks * MOE_BLOCK
    tok = jnp.broadcast_to(jnp.arange(t, dtype=jnp.int32)[:, None], (t, 2))
    slot_t = jnp.zeros((n_slots,), jnp.int32).at[dest.reshape(-1)].set(tok.reshape(-1))
    blk_e = jnp.minimum(jnp.searchsorted(pends, jnp.arange(n_blocks, dtype=jnp.int32) * MOE_BLOCK, side='right'),
                        N_EXPERTS - 1).astype(jnp.int32)
    n_used = (pends[-1:] // MOE_BLOCK).astype(jnp.int32)

    y_sorted = _experts(f, slot_t, blk_e, n_used, wg, wu, wd)
    dest3 = dest.reshape(t // ROW_TILE, ROW_TILE, 2).transpose(0, 2, 1)
    return _combine(y_sorted, dest3, h, slab, modt, 5, tpb, batch)


def _rope_tables(ctx_len, seq):
    n_rows = seq // GRID_W
    rowp = jnp.repeat(jnp.arange(n_rows, dtype=F32), GRID_W)
    colp = jnp.tile(jnp.arange(GRID_W, dtype=F32), n_rows)
    axis_dim = HEAD_DIM // 2
    inv = ROPE_THETA ** (-jnp.arange(0, axis_dim, 2, dtype=F32) / axis_dim)
    ar = rowp[:, None] * inv
    ac = colp[:, None] * inv
    cos = jnp.concatenate([jnp.cos(ar), jnp.cos(ar), jnp.cos(ac), jnp.cos(ac)], axis=-1)
    sin = jnp.concatenate([-jnp.sin(ar), jnp.sin(ar), -jnp.sin(ac), jnp.sin(ac)], axis=-1)
    cos = jnp.concatenate([jnp.ones((ctx_len, HEAD_DIM), F32), cos], axis=0)
    sin = jnp.concatenate([jnp.zeros((ctx_len, HEAD_DIM), F32), sin], axis=0)
    return cos, sin


def _pick_tile(n, candidates):
    for c in candidates:
        if n % c == 0:
            return c
    raise ValueError(f"no tile for {n}")


def kernel(x, c, ctx, c_ctx, ada_w, ada_b, norm_mix_g, norm_ffn_g, att_w_in, att_w_out, att_q_g, att_k_g, sconv_w, ssm_w_in, ssm_conv_w, ssm_conv_b, ssm_dt_bias, ssm_a_log, ssm_d, ssm_norm_w, ssm_w_out, router_grp_w, router_grp_b, router_exp_w, router_exp_b, exp_w_gate, exp_w_up, exp_w_down):
    b, s, d = x.shape
    cl = ctx.shape[1]
    assert cl == ROW_TILE and s % ROW_TILE == 0 and s % GRID_W == 0
    depth = ada_w.shape[0]
    l = cl + s
    m = b * l
    tpb = l // ROW_TILE
    d_inner = SSM_GROUPS * GROUP_W
    conv_dim = ssm_conv_w.shape[2]

    cond_t = jnp.zeros((d, LANES), F32).at[:, :b + 1].set(jnp.concatenate([c, c_ctx[None]], axis=0).T)
    mods = _ada_all(cond_t, ada_w, ada_b, b + 1).reshape(depth, SUBLANES, 6, d)
    modt = jnp.pad(mods, ((0, 0), (0, 0), (0, SUBLANES - 6), (0, 0)))

    cos_t, sin_t = _rope_tables(cl, s)
    tm = _pick_tile(m, (512, 256))
    tk = _pick_tile(l, (768, 512, 256))

    h = jnp.concatenate([ctx, x], axis=1).reshape(m, d)
    for i in range(depth):
        j = i // 2
        n = _norm_mod(h, norm_mix_g[i], modt[i], 0, 1, BF16, tpb, b)
        if i % 2 == 0:
            p = _mm(n, att_w_in[j].astype(BF16), F32, tm, 1536)
            q, k, v = _qkv_prep(p, cos_t, sin_t, att_q_g[j], att_k_g[j], tpb)
            o = _flash(q.reshape(b, l, ATT_WIDTH), k.reshape(b, l, KV_WIDTH), v.reshape(b, l, KV_WIDTH), cl, tk)
            cv = _sconv(p, sconv_w[j], tpb)
            y = jnp.concatenate([o.reshape(m, ATT_WIDTH), cv], axis=-1)
            h = _mm_res(y, att_w_out[j].astype(BF16), h, modt[i], 2, 1024, tpb, b)
        else:
            w_in = ssm_w_in[j]
            zx_w = d_inner + conv_dim
            p = _mm(n, w_in[:, :zx_w].astype(BF16), F32, tm, 1536)
            dt_raw = _mm(n, w_in[:, zx_w:].astype(BF16), F32, tm, LANES)
            xbc_g = _ssd_prep(p, ssm_conv_w[j], ssm_conv_b[j], d_inner, tpb)
            dtt_raw = dt_raw.T
            bias = ssm_dt_bias[j].reshape(-1)
            alog = ssm_a_log[j].reshape(-1)
            yf = _ssd_scan(xbc_g, dt_raw, dtt_raw, bias, alog, 0, b)
            yb = _ssd_scan(xbc_g, dt_raw, dtt_raw, bias, alog, 1, b)
            d_exp = jnp.repeat(ssm_d[j], SSM_HEAD_DIM)
            y = _ssd_finish(yf, yb, xbc_g, p, d_exp, ssm_norm_w[j])
            h = _mm_res(y, ssm_w_out[j].astype(BF16), h, modt[i], 2, 1024, tpb, b)
        f = _norm_mod(h, norm_ffn_g[i], modt[i], 3, 4, F32, tpb, b)
        h = _moe(f, h, modt[i], router_grp_w[i], router_grp_b[i], router_exp_w[i], router_exp_b[i],
                 exp_w_gate[i].astype(BF16), exp_w_up[i].astype(BF16), exp_w_down[i].astype(BF16), tpb, b)
    return h.reshape(b, l, d)[:, cl:, :]
```

```python
import functools

import jax
import jax.numpy as jnp
from jax import lax
from jax.experimental import pallas as pl
from jax.experimental.pallas import tpu as pltpu

F32 = jnp.float32
BF16 = jnp.bfloat16
HIGHEST = lax.Precision.HIGHEST

EPS = 1e-6
LANES = 128
SUBLANES = 8
ROW_TILE = 256
GRID_W = 64
ROPE_THETA = 10000.0

ATT_HEADS = 8
ATT_KV_HEADS = 2
ATT_GROUP = ATT_HEADS // ATT_KV_HEADS
HEAD_DIM = 128
ATT_WIDTH = ATT_HEADS * HEAD_DIM
KV_WIDTH = ATT_KV_HEADS * HEAD_DIM
CONV_TILE = 512
PREP_TILE = 2048

SSM_HEAD_DIM = 64
SSM_GROUPS = 8
HPG = 8
D_STATE = 128
CHUNK = 128
GROUP_W = HPG * SSM_HEAD_DIM

N_GROUPS = 4
EXP_PER_GROUP = 8
N_EXPERTS = N_GROUPS * EXP_PER_GROUP
MOE_BLOCK = 256

NEG_INF = float("-inf")
LOG2_E = 1.4426950408889634


def _cparams(sem, vmem_mb=None):
    kw = dict(dimension_semantics=sem)
    if vmem_mb is not None:
        kw["vmem_limit_bytes"] = vmem_mb * 1024 * 1024
    return pltpu.CompilerParams(**kw)


def _lane_tile(x, n):
    return jnp.concatenate([x] * n, axis=-1)


def _ada_kernel(cx_ref, w_ref, b_ref, o_ref, s_ref, *, n_cond):
    @pl.when(jnp.logical_and(pl.program_id(0) == 0, pl.program_id(1) == 0))
    def _():
        cx = cx_ref[...]
        s_ref[...] = cx * jax.nn.sigmoid(cx)

    d = cx_ref.shape[1]
    tn = o_ref.shape[2]

    def body(i, accs):
        k0 = pl.multiple_of(i * SUBLANES, SUBLANES)
        w = w_ref[0, pl.ds(k0, SUBLANES), :]
        return tuple(acc + w * _lane_tile(s_ref[r, pl.ds(k0, SUBLANES), :], tn // LANES)
                     for r, acc in enumerate(accs))

    accs = lax.fori_loop(0, d // SUBLANES, body,
                         tuple(jnp.zeros((SUBLANES, tn), F32) for _ in range(n_cond)), unroll=4)
    rows = [jnp.sum(acc, axis=0, keepdims=True) + b_ref[0] for acc in accs]
    rows += [jnp.zeros((1, tn), F32)] * (SUBLANES - n_cond)
    o_ref[0] = jnp.concatenate(rows, axis=0)


def _ada_all(cond_x, ada_w, ada_b):
    depth, d, n = ada_w.shape
    n_cond = cond_x.shape[0]
    tn = 1024
    return pl.pallas_call(
        functools.partial(_ada_kernel, n_cond=n_cond),
        out_shape=jax.ShapeDtypeStruct((depth, SUBLANES, n), F32),
        grid=(depth, n // tn),
        in_specs=[pl.BlockSpec((n_cond, d, LANES), lambda l, j: (0, 0, 0)),
                  pl.BlockSpec((1, d, tn), lambda l, j: (l, 0, j)),
                  pl.BlockSpec((1, 1, tn), lambda l, j: (l, 0, j))],
        out_specs=pl.BlockSpec((1, SUBLANES, tn), lambda l, j: (l, 0, j)),
        scratch_shapes=[pltpu.VMEM((n_cond, d, LANES), F32)],
        compiler_params=_cparams(("arbitrary", "arbitrary")),
        name="ada_mod",
    )(cond_x, ada_w, ada_b.reshape(depth, 1, n))


def _norm_mod_kernel(h_ref, g_ref, m_ref, o_ref, *, shift_row, scale_row):
    x = h_ref[...]
    y = x * lax.rsqrt(jnp.mean(x * x, axis=-1, keepdims=True) + EPS) * g_ref[...]
    sh = m_ref[0, shift_row:shift_row + 1, :]
    sc = m_ref[0, scale_row:scale_row + 1, :]
    o_ref[...] = (y * (1.0 + sc) + sh).astype(o_ref.dtype)


def _cond_of_tile(i, tiles_per_batch, batch):
    return jnp.where(i % tiles_per_batch == 0, batch, i // tiles_per_batch)


def _norm_mod(h, g, modt, shift_row, scale_row, out_dtype, tpb, batch):
    m, d = h.shape
    return pl.pallas_call(
        functools.partial(_norm_mod_kernel, shift_row=shift_row, scale_row=scale_row),
        out_shape=jax.ShapeDtypeStruct((m, d), out_dtype),
        grid=(m // ROW_TILE,),
        in_specs=[pl.BlockSpec((ROW_TILE, d), lambda i: (i, 0)),
                  pl.BlockSpec((1, d), lambda i: (0, 0)),
                  pl.BlockSpec((1, SUBLANES, d), lambda i: (_cond_of_tile(i, tpb, batch), 0, 0))],
        out_specs=pl.BlockSpec((ROW_TILE, d), lambda i: (i, 0)),
        compiler_params=_cparams(("arbitrary",)),
        name="norm_mod",
    )(h, g.reshape(1, d), modt)


def _mm_kernel(x_ref, w_ref, o_ref, wb_ref):
    @pl.when(pl.program_id(1) == 0)
    def _():
        wb_ref[...] = w_ref[0].astype(BF16)

    o_ref[...] = jnp.dot(x_ref[...], wb_ref[...], preferred_element_type=F32).astype(o_ref.dtype)


def _mm(x, w3, layer, col0, n, out_dtype, tm, tn):
    m, k = x.shape
    cb = col0 // tn
    return pl.pallas_call(
        _mm_kernel,
        out_shape=jax.ShapeDtypeStruct((m, n), out_dtype),
        grid=(n // tn, m // tm),
        in_specs=[pl.BlockSpec((tm, k), lambda j, i: (i, 0)),
                  pl.BlockSpec((1, k, tn), lambda j, i: (layer, 0, cb + j))],
        out_specs=pl.BlockSpec((tm, tn), lambda j, i: (i, j)),
        scratch_shapes=[pltpu.VMEM((k, tn), BF16)],
        compiler_params=_cparams(("arbitrary", "arbitrary")),
        name="mm",
    )(x, w3)


def _mm_res_kernel(x_ref, w_ref, r_ref, m_ref, o_ref, wb_ref, *, gate_row):
    @pl.when(pl.program_id(1) == 0)
    def _():
        wb_ref[...] = w_ref[0].astype(BF16)

    acc = jnp.dot(x_ref[...], wb_ref[...], preferred_element_type=F32)
    o_ref[...] = r_ref[...] + m_ref[0, gate_row:gate_row + 1, :] * acc


def _mm_res(x, w3, layer, res, modt, gate_row, tn, tpb, batch):
    m, k = x.shape
    n = w3.shape[2]
    return pl.pallas_call(
        functools.partial(_mm_res_kernel, gate_row=gate_row),
        out_shape=jax.ShapeDtypeStruct((m, n), F32),
        grid=(n // tn, m // ROW_TILE),
        in_specs=[pl.BlockSpec((ROW_TILE, k), lambda j, i: (i, 0)),
                  pl.BlockSpec((1, k, tn), lambda j, i: (layer, 0, j)),
                  pl.BlockSpec((ROW_TILE, tn), lambda j, i: (i, j)),
                  pl.BlockSpec((1, SUBLANES, tn), lambda j, i: (_cond_of_tile(i, tpb, batch), 0, j))],
        out_specs=pl.BlockSpec((ROW_TILE, tn), lambda j, i: (i, j)),
        scratch_shapes=[pltpu.VMEM((k, tn), BF16)],
        compiler_params=_cparams(("arbitrary", "arbitrary")),
        name="mm_res",
    )(x, w3, res, modt)


def _qkv_prep_kernel(p_ref, cos_ref, sin_ref, qg_ref, kg_ref, q_ref, k_ref, v_ref):
    cos = cos_ref[...]
    sin = sin_ref[...]
    lane = lax.broadcasted_iota(jnp.int32, cos.shape, 1)
    first_half = (lane % 64) < 32

    def norm_rope(x, g):
        y = x * lax.rsqrt(jnp.mean(x * x, axis=-1, keepdims=True) + EPS) * g
        partner = jnp.where(first_half, pltpu.roll(y, 96, axis=1), pltpu.roll(y, 32, axis=1))
        return y * cos + partner * sin

    scale = HEAD_DIM ** -0.5 * LOG2_E
    for h in range(ATT_HEADS):
        x = p_ref[:, h * HEAD_DIM:(h + 1) * HEAD_DIM]
        q_ref[:, h * HEAD_DIM:(h + 1) * HEAD_DIM] = (norm_rope(x, qg_ref[...]) * scale).astype(q_ref.dtype)
    ones = jnp.ones((p_ref.shape[0], HEAD_DIM), v_ref.dtype)
    for h in range(ATT_KV_HEADS):
        c0 = ATT_WIDTH + h * HEAD_DIM
        k_ref[:, h * HEAD_DIM:(h + 1) * HEAD_DIM] = norm_rope(p_ref[:, c0:c0 + HEAD_DIM], kg_ref[...]).astype(k_ref.dtype)
        v0 = ATT_WIDTH + KV_WIDTH + h * HEAD_DIM
        v_ref[:, 2 * h * HEAD_DIM:(2 * h + 1) * HEAD_DIM] = p_ref[:, v0:v0 + HEAD_DIM].astype(v_ref.dtype)
        v_ref[:, (2 * h + 1) * HEAD_DIM:(2 * h + 2) * HEAD_DIM] = ones


def _qkv_prep(p, cos_t, sin_t, q_g, k_g, tiles_per_batch):
    m = p.shape[0]
    w = ATT_WIDTH + 2 * KV_WIDTH
    return pl.pallas_call(
        _qkv_prep_kernel,
        out_shape=(jax.ShapeDtypeStruct((m, ATT_WIDTH), BF16),
                   jax.ShapeDtypeStruct((m, KV_WIDTH), BF16),
                   jax.ShapeDtypeStruct((m, 2 * KV_WIDTH), BF16)),
        grid=(m // ROW_TILE,),
        in_specs=[pl.BlockSpec((ROW_TILE, w), lambda i: (i, 0)),
                  pl.BlockSpec((ROW_TILE, HEAD_DIM), lambda i: (i % tiles_per_batch, 0)),
                  pl.BlockSpec((ROW_TILE, HEAD_DIM), lambda i: (i % tiles_per_batch, 0)),
                  pl.BlockSpec((1, HEAD_DIM), lambda i: (0, 0)),
                  pl.BlockSpec((1, HEAD_DIM), lambda i: (0, 0))],
        out_specs=(pl.BlockSpec((ROW_TILE, ATT_WIDTH), lambda i: (i, 0)),
                   pl.BlockSpec((ROW_TILE, KV_WIDTH), lambda i: (i, 0)),
                   pl.BlockSpec((ROW_TILE, 2 * KV_WIDTH), lambda i: (i, 0))),
        compiler_params=_cparams(("arbitrary",)),
        name="qkv_prep",
    )(p, cos_t, sin_t, q_g.reshape(1, HEAD_DIM), k_g.reshape(1, HEAD_DIM))


def _flash_kernel(q_ref, k_ref, v_ref, o_ref, m_ref, acc_ref, *, ctx_len, tk):
    qi = pl.program_id(2)
    n_keys = k_ref.shape[1]

    m_ref[...] = jnp.full(m_ref.shape, NEG_INF, F32)
    acc_ref[...] = jnp.zeros(acc_ref.shape, F32)

    def block(k, v):
        width = k.shape[0]
        for g in range(ATT_GROUP):
            q = q_ref[0, :, g * HEAD_DIM:(g + 1) * HEAD_DIM]
            s = lax.dot_general(q, k, (((1,), (1,)), ((), ())), preferred_element_type=F32)
            m_prev = m_ref[g]
            m_new = jnp.maximum(m_prev, jnp.max(s, axis=-1, keepdims=True))
            alpha = jnp.exp2(m_prev - m_new)
            p = jnp.exp2(s - _lane_tile(m_new, width // LANES))
            pv = jnp.dot(p.astype(BF16), v, preferred_element_type=F32)
            acc_ref[g] = _lane_tile(alpha, 2) * acc_ref[g] + pv
            m_ref[g] = m_new

    @pl.when(qi == 0)
    def _():
        block(k_ref[0, :ctx_len, :], v_ref[0, :ctx_len, :])

    @pl.when(qi > 0)
    def _():
        def body(j, carry):
            r0 = pl.multiple_of(j * tk, tk)
            block(k_ref[0, pl.ds(r0, tk), :], v_ref[0, pl.ds(r0, tk), :])
            return carry
        lax.fori_loop(0, n_keys // tk, body, 0)

    for g in range(ATT_GROUP):
        a = acc_ref[g]
        o_ref[0, :, g * HEAD_DIM:(g + 1) * HEAD_DIM] = (a[:, :HEAD_DIM] / a[:, HEAD_DIM:]).astype(o_ref.dtype)


def _flash(q, k, v, ctx_len, tk):
    b, l, _ = q.shape
    tq = ROW_TILE
    assert ctx_len == tq and l % tk == 0
    gw = ATT_GROUP * HEAD_DIM
    return pl.pallas_call(
        functools.partial(_flash_kernel, ctx_len=ctx_len, tk=tk),
        out_shape=jax.ShapeDtypeStruct((b, l, ATT_WIDTH), BF16),
        grid=(b, ATT_KV_HEADS, l // tq),
        in_specs=[pl.BlockSpec((1, tq, gw), lambda bi, h, qi: (bi, qi, h)),
                  pl.BlockSpec((1, l, HEAD_DIM), lambda bi, h, qi: (bi, 0, h)),
                  pl.BlockSpec((1, l, 2 * HEAD_DIM), lambda bi, h, qi: (bi, 0, h))],
        out_specs=pl.BlockSpec((1, tq, gw), lambda bi, h, qi: (bi, qi, h)),
        scratch_shapes=[pltpu.VMEM((ATT_GROUP, tq, LANES), F32),
                        pltpu.VMEM((ATT_GROUP, tq, 2 * HEAD_DIM), F32)],
        compiler_params=_cparams(("arbitrary", "arbitrary", "arbitrary")),
        name="flash",
    )(q, k, v)


def _conv3(u, prev_row, next_row, w_ref):
    rows = u.shape[0]
    row = lax.broadcasted_iota(jnp.int32, u.shape, 0)
    u_prev = jnp.where(row == 0, prev_row, pltpu.roll(u, 1, axis=0))
    u_next = jnp.where(row == rows - 1, next_row, pltpu.roll(u, rows - 1, axis=0))
    return u_prev * w_ref[0:1, :] + u * w_ref[1:2, :] + u_next * w_ref[2:3, :]


def _seq_edges(tiles_per_batch):
    tb = pl.program_id(0) % tiles_per_batch
    has_prev = tb >= 2
    has_next = jnp.logical_and(tb >= 1, tb < tiles_per_batch - 1)
    return has_prev, has_next


def _halo_specs(col_block, n_rows):
    rb = ROW_TILE // SUBLANES
    last = n_rows // SUBLANES - 1
    prev = pl.BlockSpec((SUBLANES, CONV_TILE), lambda i, j: (jnp.maximum(i * rb - 1, 0), col_block + j))
    nxt = pl.BlockSpec((SUBLANES, CONV_TILE), lambda i, j: (jnp.minimum((i + 1) * rb, last), col_block + j))
    return prev, nxt


def _sconv_kernel(gb_ref, gc_ref, gx_ref, gcp_ref, gxp_ref, gcn_ref, gxn_ref, w_ref, o_ref, *, tiles_per_batch):
    has_prev, has_next = _seq_edges(tiles_per_batch)
    u = gc_ref[...] * gx_ref[...]
    prev_row = jnp.where(has_prev, gcp_ref[SUBLANES - 1:SUBLANES, :] * gxp_ref[SUBLANES - 1:SUBLANES, :], 0.0)
    next_row = jnp.where(has_next, gcn_ref[0:1, :] * gxn_ref[0:1, :], 0.0)
    o_ref[...] = (gb_ref[...] * _conv3(u, prev_row, next_row, w_ref)).astype(o_ref.dtype)


def _sconv(p, conv_w, tiles_per_batch):
    m = p.shape[0]
    width = conv_w.shape[1]
    base = (ATT_WIDTH + 2 * KV_WIDTH) // CONV_TILE
    nb = width // CONV_TILE
    main = lambda off: pl.BlockSpec((ROW_TILE, CONV_TILE), lambda i, j: (i, off + j))
    gcp, gcn = _halo_specs(base + nb, m)
    gxp, gxn = _halo_specs(base + 2 * nb, m)
    return pl.pallas_call(
        functools.partial(_sconv_kernel, tiles_per_batch=tiles_per_batch),
        out_shape=jax.ShapeDtypeStruct((m, width), BF16),
        grid=(m // ROW_TILE, nb),
        in_specs=[main(base), main(base + nb), main(base + 2 * nb), gcp, gxp, gcn, gxn,
                  pl.BlockSpec((3, CONV_TILE), lambda i, j: (0, j))],
        out_specs=pl.BlockSpec((ROW_TILE, CONV_TILE), lambda i, j: (i, j)),
        compiler_params=_cparams(("arbitrary", "arbitrary")),
        name="sconv",
    )(p, p, p, p, p, p, p, conv_w)


def _ssd_prep_kernel(u_ref, up_ref, un_ref, w_ref, b_ref, o_ref, *, tiles_per_batch, transpose):
    has_prev, has_next = _seq_edges(tiles_per_batch)
    prev_row = jnp.where(has_prev, up_ref[SUBLANES - 1:SUBLANES, :], 0.0)
    next_row = jnp.where(has_next, un_ref[0:1, :], 0.0)
    y = _conv3(u_ref[...], prev_row, next_row, w_ref) + b_ref[...]
    y = y * jax.nn.sigmoid(y)
    for q in range(o_ref.shape[0]):
        yq = y[:, q * GROUP_W:(q + 1) * GROUP_W]
        o_ref[q] = (yq.T if transpose else yq).astype(o_ref.dtype)


def _ssd_prep(p, conv_w, conv_b, p_col, w_col, width, transpose, out_dtype, tiles_per_batch):
    m = p.shape[0]
    ct = PREP_TILE
    per = ct // GROUP_W
    pb, wb, nb = p_col // ct, w_col // ct, width // ct
    rb = ROW_TILE // SUBLANES
    last = m // SUBLANES - 1
    if transpose:
        out_shape = (width // GROUP_W, GROUP_W, m)
        out_spec = pl.BlockSpec((per, GROUP_W, ROW_TILE), lambda i, j: (j, 0, i))
    else:
        out_shape = (width // GROUP_W, m, GROUP_W)
        out_spec = pl.BlockSpec((per, ROW_TILE, GROUP_W), lambda i, j: (j, i, 0))
    return pl.pallas_call(
        functools.partial(_ssd_prep_kernel, tiles_per_batch=tiles_per_batch, transpose=transpose),
        out_shape=jax.ShapeDtypeStruct(out_shape, out_dtype),
        grid=(m // ROW_TILE, nb),
        in_specs=[pl.BlockSpec((ROW_TILE, ct), lambda i, j: (i, pb + j)),
                  pl.BlockSpec((SUBLANES, ct), lambda i, j: (jnp.maximum(i * rb - 1, 0), pb + j)),
                  pl.BlockSpec((SUBLANES, ct), lambda i, j: (jnp.minimum((i + 1) * rb, last), pb + j)),
                  pl.BlockSpec((3, ct), lambda i, j: (0, wb + j)),
                  pl.BlockSpec((1, ct), lambda i, j: (0, wb + j))],
        out_specs=out_spec,
        compiler_params=_cparams(("arbitrary", "arbitrary")),
        name="ssd_prep",
    )(p, p, p, conv_w, conv_b.reshape(1, -1))


def _softplus(x):
    return jnp.maximum(x, 0.0) + jnp.log1p(jnp.exp(-jnp.abs(x)))


def _ssd_scan_kernel(*refs, direction):
    if direction == 0:
        x_ref, b_ref, c_ref, dt_ref, dtt_ref, bias_r_ref, bias_c_ref, alog_r_ref, alog_c_ref, d_ref, \
            y_ref, h_ref, xw_ref = refs
        yf_ref = None
    else:
        x_ref, b_ref, c_ref, dt_ref, dtt_ref, bias_r_ref, bias_c_ref, alog_r_ref, alog_c_ref, yf_ref, \
            y_ref, h_ref, xw_ref = refs
        d_ref = None

    @pl.when(pl.program_id(1) == 0)
    def _():
        h_ref[...] = jnp.zeros(h_ref.shape, F32)

    dt = _softplus(dt_ref[...] + bias_r_ref[...])
    dtt = _softplus(dtt_ref[...] + bias_c_ref[...])
    dta = dt * (-jnp.exp(alog_r_ref[...]))
    dtat = dtt * (-jnp.exp(alog_c_ref[...]))
    row = lax.broadcasted_iota(jnp.int32, (CHUNK, CHUNK), 0)
    col = lax.broadcasted_iota(jnp.int32, (CHUNK, CHUNK), 1)
    mask = (row >= col) if direction == 0 else (row <= col)
    mask_t = (row <= col) if direction == 0 else (row >= col)
    tri = mask.astype(F32)
    a_all = jnp.dot(tri, dta, precision=HIGHEST, preferred_element_type=F32)
    at_all = lax.dot_general(dtat, tri, (((1,), (1,)), ((), ())), precision=HIGHEST,
                             preferred_element_type=F32)
    a_tot = jnp.sum(dtat, axis=1, keepdims=True)
    nt = (((1,), (1,)), ((), ()))

    for g in range(SSM_GROUPS):
        sl = slice((g % 4) * D_STATE, (g % 4 + 1) * D_STATE)
        bg = b_ref[g // 4, :, sl]
        cg = c_ref[g // 4, :, sl]
        cb_t = lax.dot_general(bg, cg, nt, preferred_element_type=F32)
        y_off = lax.dot_general(h_ref[g].astype(BF16), cg, nt, preferred_element_type=F32)
        for k in range(HPG):
            hd = direction * SSM_GROUPS * HPG + g * HPG + k
            ch = slice(k * SSM_HEAD_DIM, (k + 1) * SSM_HEAD_DIM)
            a_col = a_all[:, hd:hd + 1]
            a_row = at_all[hd:hd + 1, :]
            a_last = a_tot[hd:hd + 1, :]
            decay_t = jnp.exp(jnp.where(mask_t, a_row - a_col, NEG_INF))
            xk = x_ref[g, ch, :]
            xdt = xk * dtt[hd:hd + 1, :]
            y = jnp.dot(xdt.astype(BF16), (cb_t * decay_t).astype(BF16), preferred_element_type=F32)
            y = y + y_off[ch, :] * jnp.exp(a_row)
            if direction == 0:
                y = y + d_ref[hd:hd + 1, :] * xk
            else:
                y = y + yf_ref[g, ch, :]
            y_ref[g, ch, :] = y
            xw_ref[ch, :] = (xdt * jnp.exp(a_last - a_row)).astype(BF16)
            h_ref[g, ch, :] = h_ref[g, ch, :] * jnp.exp(a_last)
        h_ref[g] = h_ref[g] + jnp.dot(xw_ref[...], bg, preferred_element_type=F32)


def _ssd_scan(x_t, bc, dt_raw, dtt_raw, dt_bias, a_log, extra, direction, batch):
    _, _, m = x_t.shape
    chunks = m // CHUNK // batch
    ctx_chunks = ROW_TILE // CHUNK

    def rb(bi, t):
        if direction == 0:
            c = t
        else:
            c = jnp.where(t < ctx_chunks, ctx_chunks - 1 - t, chunks - 1 + ctx_chunks - t)
        return bi * chunks + c

    nh = dt_raw.shape[1]
    rowv = lambda v: v.reshape(1, nh)
    colv = lambda v: v.reshape(nh, 1)
    xspec = pl.BlockSpec((SSM_GROUPS, GROUP_W, CHUNK), lambda bi, t: (0, 0, rb(bi, t)))
    const = lambda shape: pl.BlockSpec(shape, lambda bi, t: (0, 0))
    extra_spec = const((nh, 1)) if direction == 0 else xspec
    return pl.pallas_call(
        functools.partial(_ssd_scan_kernel, direction=direction),
        out_shape=jax.ShapeDtypeStruct(x_t.shape, F32),
        grid=(batch, chunks),
        in_specs=[xspec,
                  pl.BlockSpec((2, CHUNK, GROUP_W), lambda bi, t: (2 * direction, rb(bi, t), 0)),
                  pl.BlockSpec((2, CHUNK, GROUP_W), lambda bi, t: (2 * direction + 1, rb(bi, t), 0)),
                  pl.BlockSpec((CHUNK, nh), lambda bi, t: (rb(bi, t), 0)),
                  pl.BlockSpec((nh, CHUNK), lambda bi, t: (0, rb(bi, t))),
                  const((1, nh)), const((nh, 1)), const((1, nh)), const((nh, 1)), extra_spec],
        out_specs=xspec,
        scratch_shapes=[pltpu.VMEM((SSM_GROUPS, GROUP_W, D_STATE), F32),
                        pltpu.VMEM((GROUP_W, CHUNK), BF16)],
        compiler_params=_cparams(("arbitrary", "arbitrary")),
        name=f"ssd_scan{direction}",
    )(x_t, bc, bc, dt_raw, dtt_raw, rowv(dt_bias), colv(dt_bias), rowv(a_log), colv(a_log), extra)


def _ssd_finish_kernel(y_ref, z_ref, nw_ref, o_ref):
    for g in range(SSM_GROUPS):
        cols = slice(g * GROUP_W, (g + 1) * GROUP_W)
        z = z_ref[:, cols]
        y = y_ref[g].T * (z * jax.nn.sigmoid(z))
        y = y * lax.rsqrt(jnp.mean(y * y, axis=-1, keepdims=True) + EPS)
        o_ref[:, cols] = (y * nw_ref[:, cols]).astype(o_ref.dtype)


def _ssd_finish(y_t, p, norm_w):
    _, _, m = y_t.shape
    d_inner = SSM_GROUPS * GROUP_W
    return pl.pallas_call(
        _ssd_finish_kernel,
        out_shape=jax.ShapeDtypeStruct((m, d_inner), BF16),
        grid=(m // ROW_TILE,),
        in_specs=[pl.BlockSpec((SSM_GROUPS, GROUP_W, ROW_TILE), lambda i: (0, 0, i)),
                  pl.BlockSpec((ROW_TILE, d_inner), lambda i: (i, 0)),
                  pl.BlockSpec((1, d_inner), lambda i: (0, 0))],
        out_specs=pl.BlockSpec((ROW_TILE, d_inner), lambda i: (i, 0)),
        compiler_params=_cparams(("arbitrary",)),
        name="ssd_finish",
    )(y_t, p, norm_w.reshape(1, d_inner))


def _router_kernel(f_ref, w_ref, b_ref, o_ref, cnt_ref, carry_ref):
    @pl.when(pl.program_id(0) == 0)
    def _():
        carry_ref[...] = jnp.zeros(carry_ref.shape, F32)

    logits = jnp.dot(f_ref[...], w_ref[...], precision=HIGHEST, preferred_element_type=F32) + b_ref[...]
    rows = logits.shape[0]
    lane = lax.broadcasted_iota(jnp.int32, logits.shape, 1)
    big = jnp.int32(1 << 20)

    def first_lane(cond):
        return jnp.min(jnp.where(cond, lane, big), axis=-1, keepdims=True)

    gmask = lane < N_GROUPS
    gl = jnp.where(gmask, logits, NEG_INF)
    ge = jnp.exp(gl - jnp.max(gl, axis=-1, keepdims=True))
    g_prob = ge / jnp.sum(ge, axis=-1, keepdims=True)
    g_p = jnp.max(g_prob, axis=-1, keepdims=True)
    g_sel = first_lane(jnp.logical_and(gmask, g_prob == g_p))
    lo = N_GROUPS + EXP_PER_GROUP * g_sel
    emask = jnp.logical_and(lane >= lo, lane < lo + EXP_PER_GROUP)
    el = jnp.where(emask, logits, NEG_INF)
    ee = jnp.exp(el - jnp.max(el, axis=-1, keepdims=True))
    e_prob = jnp.where(emask, ee / jnp.sum(ee, axis=-1, keepdims=True), -1.0)
    p1 = jnp.max(e_prob, axis=-1, keepdims=True)
    i1 = first_lane(e_prob == p1)
    e_rest = jnp.where(lane == i1, -1.0, e_prob)
    p2 = jnp.max(e_rest, axis=-1, keepdims=True)
    i2 = first_lane(e_rest == p2)
    denom = p1 + p2
    gate1 = g_p * p1 / denom
    gate2 = g_p * p2 / denom
    e1 = i1 - N_GROUPS
    e2 = i2 - N_GROUPS

    oh1 = lane == e1
    oh2 = lane == e2
    ohs = jnp.logical_or(oh1, oh2).astype(F32)
    r_i = lax.broadcasted_iota(jnp.int32, (rows, rows), 0)
    c_i = lax.broadcasted_iota(jnp.int32, (rows, rows), 1)
    before = (r_i > c_i).astype(F32).astype(BF16)
    seen = jnp.dot(before, ohs.astype(BF16), preferred_element_type=F32) + carry_ref[...]
    rank1 = jnp.sum(jnp.where(oh1, seen, 0.0), axis=-1, keepdims=True)
    rank2 = jnp.sum(jnp.where(oh2, seen, 0.0), axis=-1, keepdims=True)
    carry_ref[...] = carry_ref[...] + jnp.sum(ohs, axis=0, keepdims=True)

    slab = jnp.zeros(logits.shape, F32)
    for idx, val in enumerate((e1.astype(F32), e2.astype(F32), gate1, gate2, rank1, rank2)):
        slab = jnp.where(lane == idx, val, slab)
    o_ref[...] = slab
    cnt_ref[...] = jnp.broadcast_to(carry_ref[...], cnt_ref.shape)


def _router(f, w, b):
    t, d = f.shape
    return pl.pallas_call(
        _router_kernel,
        out_shape=(jax.ShapeDtypeStruct((t, LANES), F32), jax.ShapeDtypeStruct((SUBLANES, LANES), F32)),
        grid=(t // ROW_TILE,),
        in_specs=[pl.BlockSpec((ROW_TILE, d), lambda i: (i, 0)),
                  pl.BlockSpec((d, LANES), lambda i: (0, 0)),
                  pl.BlockSpec((1, LANES), lambda i: (0, 0))],
        out_specs=(pl.BlockSpec((ROW_TILE, LANES), lambda i: (i, 0)),
                   pl.BlockSpec((SUBLANES, LANES), lambda i: (0, 0))),
        scratch_shapes=[pltpu.VMEM((1, LANES), F32)],
        compiler_params=_cparams(("arbitrary",)),
        name="router",
    )(f, w, b)


def _row_copy(src_hbm, dst_ref, sem, src_row, dst_row):
    return pltpu.make_async_copy(src_hbm.at[pl.ds(src_row, 1)], dst_ref.at[pl.ds(dst_row, 1)], sem)


def _gather_rows_loop(idx_ref, k, src_hbm, dst_ref, sem, n_rows):
    def body(r, carry):
        _row_copy(src_hbm, dst_ref, sem, idx_ref[0, k, r], r).start()
        return carry

    lax.fori_loop(0, n_rows, body, 0)


def _gather_rows_unrolled(idx_ref, k, src_hbm, dst_ref, sem, n_rows):
    for r in range(n_rows):
        _row_copy(src_hbm, dst_ref, sem, idx_ref[0, k, r], r).start()


def _wait_rows(src_hbm, dst_ref, sem, n_rows):
    pltpu.make_async_copy(src_hbm.at[pl.ds(0, n_rows)], dst_ref, sem).wait()


def _expert_kernel(blk_e_ref, cur_ref, nxt_ref, f_hbm, wg_ref, wu_ref, wd_ref, o_ref, xbuf, xb_ref, sems):
    i = pl.program_id(0)
    slot = i % 2

    @pl.when(i == 0)
    def _():
        _gather_rows_loop(cur_ref, 0, f_hbm, xbuf.at[0], sems.at[0], MOE_BLOCK)

    _wait_rows(f_hbm, xbuf.at[slot], sems.at[slot], MOE_BLOCK)
    xb_ref[...] = xbuf[slot].astype(BF16)
    _gather_rows_unrolled(nxt_ref, 0, f_hbm, xbuf.at[1 - slot], sems.at[1 - slot], MOE_BLOCK)
    x = xb_ref[...]
    hg = jnp.dot(x, wg_ref[0, 0], preferred_element_type=F32)
    hu = jnp.dot(x, wu_ref[0, 0], preferred_element_type=F32)
    act = (hg * jax.nn.sigmoid(hg) * hu).astype(BF16)
    o_ref[...] = jnp.dot(act, wd_ref[0, 0], preferred_element_type=F32)

    @pl.when(i == pl.num_programs(0) - 1)
    def _():
        _wait_rows(f_hbm, xbuf.at[1 - slot], sems.at[1 - slot], MOE_BLOCK)


def _experts(f, slot_t, blk_e, wg, wu, wd, layer):
    t, d = f.shape
    n_blocks = blk_e.shape[0]
    de = wg.shape[3]
    slots3 = slot_t.reshape(n_blocks, 1, MOE_BLOCK)
    grid_spec = pltpu.PrefetchScalarGridSpec(
        num_scalar_prefetch=1,
        grid=(n_blocks,),
        in_specs=[pl.BlockSpec((1, 1, MOE_BLOCK), lambda i, be: (i, 0, 0), memory_space=pltpu.SMEM),
                  pl.BlockSpec((1, 1, MOE_BLOCK), lambda i, be: (jnp.minimum(i + 1, n_blocks - 1), 0, 0),
                               memory_space=pltpu.SMEM),
                  pl.BlockSpec(memory_space=pl.ANY),
                  pl.BlockSpec((1, 1, d, de), lambda i, be: (layer, be[i], 0, 0)),
                  pl.BlockSpec((1, 1, d, de), lambda i, be: (layer, be[i], 0, 0)),
                  pl.BlockSpec((1, 1, de, d), lambda i, be: (layer, be[i], 0, 0))],
        out_specs=pl.BlockSpec((MOE_BLOCK, d), lambda i, be: (i, 0)),
        scratch_shapes=[pltpu.VMEM((2, MOE_BLOCK, d), F32), pltpu.VMEM((MOE_BLOCK, d), BF16),
                        pltpu.SemaphoreType.DMA((2,))],
    )
    return pl.pallas_call(
        _expert_kernel,
        out_shape=jax.ShapeDtypeStruct((n_blocks * MOE_BLOCK, d), F32),
        grid_spec=grid_spec,
        compiler_params=_cparams(("arbitrary",)),
        name="experts",
    )(blk_e, slots3, slots3, f, wg, wu, wd)


def _combine_kernel(cur_ref, nxt_ref, y_hbm, h_ref, slab_ref, m_ref, o_ref, ybuf, sems, *, gate_row):
    i = pl.program_id(0)
    slot = i % 2

    @pl.when(i == 0)
    def _():
        for k in range(2):
            _gather_rows_loop(cur_ref, k, y_hbm, ybuf.at[0, k], sems.at[0, k], ROW_TILE)

    for k in range(2):
        _wait_rows(y_hbm, ybuf.at[slot, k], sems.at[slot, k], ROW_TILE)
    for k in range(2):
        _gather_rows_unrolled(nxt_ref, k, y_hbm, ybuf.at[1 - slot, k], sems.at[1 - slot, k], ROW_TILE)
    slab = slab_ref[...]
    y = slab[:, 2:3] * ybuf[slot, 0] + slab[:, 3:4] * ybuf[slot, 1]
    o_ref[...] = h_ref[...] + m_ref[0, gate_row:gate_row + 1, :] * y

    @pl.when(i == pl.num_programs(0) - 1)
    def _():
        for k in range(2):
            _wait_rows(y_hbm, ybuf.at[1 - slot, k], sems.at[1 - slot, k], ROW_TILE)


def _combine(y_sorted, dest3, h, slab, modt, gate_row, tpb, batch):
    t, d = h.shape
    nt = t // ROW_TILE
    return pl.pallas_call(
        functools.partial(_combine_kernel, gate_row=gate_row),
        out_shape=jax.ShapeDtypeStruct((t, d), F32),
        grid=(nt,),
        in_specs=[pl.BlockSpec((1, 2, ROW_TILE), lambda i: (i, 0, 0), memory_space=pltpu.SMEM),
                  pl.BlockSpec((1, 2, ROW_TILE), lambda i: (jnp.minimum(i + 1, nt - 1), 0, 0),
                               memory_space=pltpu.SMEM),
                  pl.BlockSpec(memory_space=pl.ANY),
                  pl.BlockSpec((ROW_TILE, d), lambda i: (i, 0)),
                  pl.BlockSpec((ROW_TILE, LANES), lambda i: (i, 0)),
                  pl.BlockSpec((1, SUBLANES, d), lambda i: (_cond_of_tile(i, tpb, batch), 0, 0))],
        out_specs=pl.BlockSpec((ROW_TILE, d), lambda i: (i, 0)),
        scratch_shapes=[pltpu.VMEM((2, 2, ROW_TILE, d), F32), pltpu.SemaphoreType.DMA((2, 2))],
        compiler_params=_cparams(("arbitrary",)),
        name="moe_combine",
    )(dest3, dest3, y_sorted, h, slab, modt)


def _moe(f, h, modt, w_grp, b_grp, w_exp, b_exp, wg, wu, wd, layer, tpb, batch):
    t, d = f.shape
    n_r = N_GROUPS + N_EXPERTS
    w_r = jnp.zeros((d, LANES), F32).at[:, :n_r].set(jnp.concatenate([w_grp, w_exp], axis=1))
    b_r = jnp.zeros((1, LANES), F32).at[0, :n_r].set(jnp.concatenate([b_grp, b_exp]))
    slab, cnt = _router(f, w_r, b_r)

    counts = cnt[0, :N_EXPERTS].astype(jnp.int32)
    pcounts = (counts + MOE_BLOCK - 1) // MOE_BLOCK * MOE_BLOCK
    pends = jnp.cumsum(pcounts)
    pstarts = pends - pcounts
    eid = slab[:, 0:2].astype(jnp.int32)
    rank = slab[:, 4:6].astype(jnp.int32)
    dest = pstarts[eid] + rank
    n_blocks = -(-2 * t // MOE_BLOCK) + N_EXPERTS
    n_slots = n_blocks * MOE_BLOCK
    tok = jnp.broadcast_to(jnp.arange(t, dtype=jnp.int32)[:, None], (t, 2))
    slot_t = jnp.zeros((n_slots,), jnp.int32).at[dest.reshape(-1)].set(tok.reshape(-1))
    blk_e = jnp.minimum(jnp.searchsorted(pends, jnp.arange(n_blocks, dtype=jnp.int32) * MOE_BLOCK, side='right'),
                        N_EXPERTS - 1).astype(jnp.int32)

    y_sorted = _experts(f, slot_t, blk_e, wg, wu, wd, layer)
    dest3 = dest.reshape(t // ROW_TILE, ROW_TILE, 2).transpose(0, 2, 1)
    return _combine(y_sorted, dest3, h, slab, modt, 5, tpb, batch)


def _rope_tables(ctx_len, seq):
    n_rows = seq // GRID_W
    rowp = jnp.repeat(jnp.arange(n_rows, dtype=F32), GRID_W)
    colp = jnp.tile(jnp.arange(GRID_W, dtype=F32), n_rows)
    axis_dim = HEAD_DIM // 2
    inv = ROPE_THETA ** (-jnp.arange(0, axis_dim, 2, dtype=F32) / axis_dim)
    ar = rowp[:, None] * inv
    ac = colp[:, None] * inv
    cos = jnp.concatenate([jnp.cos(ar), jnp.cos(ar), jnp.cos(ac), jnp.cos(ac)], axis=-1)
    sin = jnp.concatenate([-jnp.sin(ar), jnp.sin(ar), -jnp.sin(ac), jnp.sin(ac)], axis=-1)
    cos = jnp.concatenate([jnp.ones((ctx_len, HEAD_DIM), F32), cos], axis=0)
    sin = jnp.concatenate([jnp.zeros((ctx_len, HEAD_DIM), F32), sin], axis=0)
    return cos, sin


def _pick_tile(n, candidates):
    for c in candidates:
        if n % c == 0:
            return c
    raise ValueError(f"no tile for {n}")


def kernel(x, c, ctx, c_ctx, ada_w, ada_b, norm_mix_g, norm_ffn_g, att_w_in, att_w_out, att_q_g, att_k_g, sconv_w, ssm_w_in, ssm_conv_w, ssm_conv_b, ssm_dt_bias, ssm_a_log, ssm_d, ssm_norm_w, ssm_w_out, router_grp_w, router_grp_b, router_exp_w, router_exp_b, exp_w_gate, exp_w_up, exp_w_down):
    b, s, d = x.shape
    cl = ctx.shape[1]
    assert cl == ROW_TILE and s % ROW_TILE == 0 and s % GRID_W == 0
    depth = ada_w.shape[0]
    l = cl + s
    m = b * l
    tpb = l // ROW_TILE
    d_inner = SSM_GROUPS * GROUP_W
    conv_dim = ssm_conv_w.shape[2]

    cond = jnp.concatenate([c, c_ctx[None]], axis=0)
    cond_x = jnp.broadcast_to(cond[:, :, None], (b + 1, d, LANES))
    mods = _ada_all(cond_x, ada_w, ada_b).reshape(depth, SUBLANES, 6, d)
    modt = jnp.pad(mods, ((0, 0), (0, 0), (0, SUBLANES - 6), (0, 0)))

    cos_t, sin_t = _rope_tables(cl, s)
    tm = _pick_tile(m, (512, 256))
    tk = _pick_tile(l, (1408, 768, 512, 256))

    wg, wu, wd = exp_w_gate.astype(BF16), exp_w_up.astype(BF16), exp_w_down.astype(BF16)
    h = jnp.concatenate([ctx, x], axis=1).reshape(m, d)
    for i in range(depth):
        j = i // 2
        n = _norm_mod(h, norm_mix_g[i], modt[i], 0, 1, BF16, tpb, b)
        if i % 2 == 0:
            p = _mm(n, att_w_in, j, 0, att_w_in.shape[2], F32, tm, 1536)
            q, k, v = _qkv_prep(p, cos_t, sin_t, att_q_g[j], att_k_g[j], tpb)
            o = _flash(q.reshape(b, l, ATT_WIDTH), k.reshape(b, l, KV_WIDTH), v.reshape(b, l, 2 * KV_WIDTH), cl, tk)
            cv = _sconv(p, sconv_w[j], tpb)
            y = jnp.concatenate([o.reshape(m, ATT_WIDTH), cv], axis=-1)
            h = _mm_res(y, att_w_out, j, h, modt[i], 2, 1024, tpb, b)
        else:
            zx_w = d_inner + conv_dim
            p = _mm(n, ssm_w_in, j, 0, zx_w, F32, tm, 1536)
            dt_raw = _mm(n, ssm_w_in, j, zx_w, ssm_w_in.shape[2] - zx_w, F32, tm, LANES)
            x_t = _ssd_prep(p, ssm_conv_w[j], ssm_conv_b[j], d_inner, 0, d_inner, True, F32, tpb)
            bc = _ssd_prep(p, ssm_conv_w[j], ssm_conv_b[j], 2 * d_inner, d_inner, conv_dim - d_inner, False, BF16, tpb)
            dtt_raw = dt_raw.T
            bias = ssm_dt_bias[j].reshape(-1)
            alog = ssm_a_log[j].reshape(-1)
            d_col = jnp.tile(ssm_d[j], 2).reshape(-1, 1)
            yf = _ssd_scan(x_t, bc, dt_raw, dtt_raw, bias, alog, d_col, 0, b)
            y_t = _ssd_scan(x_t, bc, dt_raw, dtt_raw, bias, alog, yf, 1, b)
            y = _ssd_finish(y_t, p, ssm_norm_w[j])
            h = _mm_res(y, ssm_w_out, j, h, modt[i], 2, 512, tpb, b)
        f = _norm_mod(h, norm_ffn_g[i], modt[i], 3, 4, F32, tpb, b)
        h = _moe(f, h, modt[i], router_grp_w[i], router_grp_b[i], router_exp_w[i], router_exp_b[i],
                 wg, wu, wd, i, tpb, b)
    return h.reshape(b, l, d)[:, cl:, :]
```

```python
import functools

import jax
import jax.numpy as jnp
from jax import lax
from jax.experimental import pallas as pl
from jax.experimental.pallas import tpu as pltpu

F32 = jnp.float32
BF16 = jnp.bfloat16
HIGHEST = lax.Precision.HIGHEST

EPS = 1e-6
LANES = 128
SUBLANES = 8
ROW_TILE = 256
GRID_W = 64
ROPE_THETA = 10000.0

ATT_HEADS = 8
ATT_KV_HEADS = 2
ATT_GROUP = ATT_HEADS // ATT_KV_HEADS
HEAD_DIM = 128
ATT_WIDTH = ATT_HEADS * HEAD_DIM
KV_WIDTH = ATT_KV_HEADS * HEAD_DIM
CONV_TILE = 512
PREP_TILE = 2048

SSM_HEAD_DIM = 64
SSM_GROUPS = 8
HPG = 8
D_STATE = 128
CHUNK = 128
GROUP_W = HPG * SSM_HEAD_DIM

N_GROUPS = 4
EXP_PER_GROUP = 8
N_EXPERTS = N_GROUPS * EXP_PER_GROUP
MOE_BLOCK = 256

NEG_INF = float("-inf")
LOG2_E = 1.4426950408889634


def _cparams(sem, vmem_mb=None):
    kw = dict(dimension_semantics=sem)
    if vmem_mb is not None:
        kw["vmem_limit_bytes"] = vmem_mb * 1024 * 1024
    return pltpu.CompilerParams(**kw)


def _lane_tile(x, n):
    return jnp.concatenate([x] * n, axis=-1)


def _ada_kernel(cx_ref, w_ref, b_ref, o_ref, s_ref, *, n_cond):
    @pl.when(jnp.logical_and(pl.program_id(0) == 0, pl.program_id(1) == 0))
    def _():
        cx = cx_ref[...]
        s_ref[...] = cx * jax.nn.sigmoid(cx)

    d = cx_ref.shape[1]
    tn = o_ref.shape[2]

    def body(i, accs):
        k0 = pl.multiple_of(i * SUBLANES, SUBLANES)
        w = w_ref[0, pl.ds(k0, SUBLANES), :]
        return tuple(acc + w * _lane_tile(s_ref[r, pl.ds(k0, SUBLANES), :], tn // LANES)
                     for r, acc in enumerate(accs))

    accs = lax.fori_loop(0, d // SUBLANES, body,
                         tuple(jnp.zeros((SUBLANES, tn), F32) for _ in range(n_cond)), unroll=4)
    rows = [jnp.sum(acc, axis=0, keepdims=True) + b_ref[0] for acc in accs]
    rows += [jnp.zeros((1, tn), F32)] * (SUBLANES - n_cond)
    o_ref[0] = jnp.concatenate(rows, axis=0)


def _ada_all(cond_x, ada_w, ada_b):
    depth, d, n = ada_w.shape
    n_cond = cond_x.shape[0]
    tn = 1024
    return pl.pallas_call(
        functools.partial(_ada_kernel, n_cond=n_cond),
        out_shape=jax.ShapeDtypeStruct((depth, SUBLANES, n), F32),
        grid=(depth, n // tn),
        in_specs=[pl.BlockSpec((n_cond, d, LANES), lambda l, j: (0, 0, 0)),
                  pl.BlockSpec((1, d, tn), lambda l, j: (l, 0, j)),
                  pl.BlockSpec((1, 1, tn), lambda l, j: (l, 0, j))],
        out_specs=pl.BlockSpec((1, SUBLANES, tn), lambda l, j: (l, 0, j)),
        scratch_shapes=[pltpu.VMEM((n_cond, d, LANES), F32)],
        compiler_params=_cparams(("arbitrary", "arbitrary")),
        name="ada_mod",
    )(cond_x, ada_w, ada_b.reshape(depth, 1, n))


def _norm_mod_kernel(h_ref, g_ref, m_ref, o_ref, *, shift_row, scale_row):
    x = h_ref[...]
    y = x * lax.rsqrt(jnp.mean(x * x, axis=-1, keepdims=True) + EPS) * g_ref[...]
    sh = m_ref[0, shift_row:shift_row + 1, :]
    sc = m_ref[0, scale_row:scale_row + 1, :]
    o_ref[...] = (y * (1.0 + sc) + sh).astype(o_ref.dtype)


def _cond_of_tile(i, tiles_per_batch, batch):
    return jnp.where(i % tiles_per_batch == 0, batch, i // tiles_per_batch)


def _norm_mod(h, g, modt, shift_row, scale_row, out_dtype, tpb, batch):
    m, d = h.shape
    return pl.pallas_call(
        functools.partial(_norm_mod_kernel, shift_row=shift_row, scale_row=scale_row),
        out_shape=jax.ShapeDtypeStruct((m, d), out_dtype),
        grid=(m // ROW_TILE,),
        in_specs=[pl.BlockSpec((ROW_TILE, d), lambda i: (i, 0)),
                  pl.BlockSpec((1, d), lambda i: (0, 0)),
                  pl.BlockSpec((1, SUBLANES, d), lambda i: (_cond_of_tile(i, tpb, batch), 0, 0))],
        out_specs=pl.BlockSpec((ROW_TILE, d), lambda i: (i, 0)),
        compiler_params=_cparams(("arbitrary",)),
        name="norm_mod",
    )(h, g.reshape(1, d), modt)


def _mm_kernel(x_ref, w_ref, o_ref, wb_ref):
    @pl.when(pl.program_id(1) == 0)
    def _():
        wb_ref[...] = w_ref[0].astype(BF16)

    o_ref[...] = jnp.dot(x_ref[...], wb_ref[...], preferred_element_type=F32).astype(o_ref.dtype)


def _mm(x, w3, layer, col0, n, out_dtype, tm, tn):
    m, k = x.shape
    cb = col0 // tn
    return pl.pallas_call(
        _mm_kernel,
        out_shape=jax.ShapeDtypeStruct((m, n), out_dtype),
        grid=(n // tn, m // tm),
        in_specs=[pl.BlockSpec((tm, k), lambda j, i: (i, 0)),
                  pl.BlockSpec((1, k, tn), lambda j, i: (layer, 0, cb + j))],
        out_specs=pl.BlockSpec((tm, tn), lambda j, i: (i, j)),
        scratch_shapes=[pltpu.VMEM((k, tn), BF16)],
        compiler_params=_cparams(("arbitrary", "arbitrary")),
        name="mm",
    )(x, w3)


def _mm_res_kernel(x_ref, w_ref, r_ref, m_ref, o_ref, *, gate_row):
    acc = jnp.dot(x_ref[...], w_ref[0], preferred_element_type=F32)
    o_ref[...] = r_ref[...] + m_ref[0, gate_row:gate_row + 1, :] * acc


def _mm_res(x, w3, layer, res, modt, gate_row, tn, tpb, batch):
    m, k = x.shape
    n = w3.shape[2]
    return pl.pallas_call(
        functools.partial(_mm_res_kernel, gate_row=gate_row),
        out_shape=jax.ShapeDtypeStruct((m, n), F32),
        grid=(n // tn, m // ROW_TILE),
        in_specs=[pl.BlockSpec((ROW_TILE, k), lambda j, i: (i, 0)),
                  pl.BlockSpec((1, k, tn), lambda j, i: (layer, 0, j)),
                  pl.BlockSpec((ROW_TILE, tn), lambda j, i: (i, j)),
                  pl.BlockSpec((1, SUBLANES, tn), lambda j, i: (_cond_of_tile(i, tpb, batch), 0, j))],
        out_specs=pl.BlockSpec((ROW_TILE, tn), lambda j, i: (i, j)),
        compiler_params=_cparams(("arbitrary", "arbitrary")),
        name="mm_res",
    )(x, w3, res, modt)


def _qkv_prep_kernel(p_ref, cos_ref, sin_ref, qg_ref, kg_ref, q_ref, k_ref, v_ref):
    cos = cos_ref[...]
    sin = sin_ref[...]
    lane = lax.broadcasted_iota(jnp.int32, cos.shape, 1)
    first_half = (lane % 64) < 32

    def norm_rope(x, g):
        y = x * lax.rsqrt(jnp.mean(x * x, axis=-1, keepdims=True) + EPS) * g
        partner = jnp.where(first_half, pltpu.roll(y, 96, axis=1), pltpu.roll(y, 32, axis=1))
        return y * cos + partner * sin

    scale = HEAD_DIM ** -0.5 * LOG2_E
    for h in range(ATT_HEADS):
        x = p_ref[:, h * HEAD_DIM:(h + 1) * HEAD_DIM]
        q_ref[:, h * HEAD_DIM:(h + 1) * HEAD_DIM] = (norm_rope(x, qg_ref[...]) * scale).astype(q_ref.dtype)
    ones = jnp.ones((p_ref.shape[0], HEAD_DIM), v_ref.dtype)
    for h in range(ATT_KV_HEADS):
        c0 = ATT_WIDTH + h * HEAD_DIM
        k_ref[:, h * HEAD_DIM:(h + 1) * HEAD_DIM] = norm_rope(p_ref[:, c0:c0 + HEAD_DIM], kg_ref[...]).astype(k_ref.dtype)
        v0 = ATT_WIDTH + KV_WIDTH + h * HEAD_DIM
        v_ref[:, 2 * h * HEAD_DIM:(2 * h + 1) * HEAD_DIM] = p_ref[:, v0:v0 + HEAD_DIM].astype(v_ref.dtype)
        v_ref[:, (2 * h + 1) * HEAD_DIM:(2 * h + 2) * HEAD_DIM] = ones


def _qkv_prep(p, cos_t, sin_t, q_g, k_g, tiles_per_batch):
    m = p.shape[0]
    w = ATT_WIDTH + 2 * KV_WIDTH
    return pl.pallas_call(
        _qkv_prep_kernel,
        out_shape=(jax.ShapeDtypeStruct((m, ATT_WIDTH), BF16),
                   jax.ShapeDtypeStruct((m, KV_WIDTH), BF16),
                   jax.ShapeDtypeStruct((m, 2 * KV_WIDTH), BF16)),
        grid=(m // ROW_TILE,),
        in_specs=[pl.BlockSpec((ROW_TILE, w), lambda i: (i, 0)),
                  pl.BlockSpec((ROW_TILE, HEAD_DIM), lambda i: (i % tiles_per_batch, 0)),
                  pl.BlockSpec((ROW_TILE, HEAD_DIM), lambda i: (i % tiles_per_batch, 0)),
                  pl.BlockSpec((1, HEAD_DIM), lambda i: (0, 0)),
                  pl.BlockSpec((1, HEAD_DIM), lambda i: (0, 0))],
        out_specs=(pl.BlockSpec((ROW_TILE, ATT_WIDTH), lambda i: (i, 0)),
                   pl.BlockSpec((ROW_TILE, KV_WIDTH), lambda i: (i, 0)),
                   pl.BlockSpec((ROW_TILE, 2 * KV_WIDTH), lambda i: (i, 0))),
        compiler_params=_cparams(("arbitrary",)),
        name="qkv_prep",
    )(p, cos_t, sin_t, q_g.reshape(1, HEAD_DIM), k_g.reshape(1, HEAD_DIM))


def _flash_kernel(q_ref, k_ref, v_ref, o_ref, m_ref, acc_ref, *, ctx_len, tk):
    qi = pl.program_id(2)
    n_keys = k_ref.shape[1]

    m_ref[...] = jnp.full(m_ref.shape, NEG_INF, F32)
    acc_ref[...] = jnp.zeros(acc_ref.shape, F32)

    def block(k, v):
        width = k.shape[0]
        for g in range(ATT_GROUP):
            q = q_ref[0, :, g * HEAD_DIM:(g + 1) * HEAD_DIM]
            s = lax.dot_general(q, k, (((1,), (1,)), ((), ())), preferred_element_type=F32)
            m_prev = m_ref[g]
            m_new = jnp.maximum(m_prev, jnp.max(s, axis=-1, keepdims=True))
            alpha = jnp.exp2(m_prev - m_new)
            p = jnp.exp2(s - _lane_tile(m_new, width // LANES))
            pv = jnp.dot(p.astype(BF16), v, preferred_element_type=F32)
            acc_ref[g] = _lane_tile(alpha, 2) * acc_ref[g] + pv
            m_ref[g] = m_new

    @pl.when(qi == 0)
    def _():
        block(k_ref[0, :ctx_len, :], v_ref[0, :ctx_len, :])

    @pl.when(qi > 0)
    def _():
        def body(j, carry):
            r0 = pl.multiple_of(j * tk, tk)
            block(k_ref[0, pl.ds(r0, tk), :], v_ref[0, pl.ds(r0, tk), :])
            return carry
        lax.fori_loop(0, n_keys // tk, body, 0)

    for g in range(ATT_GROUP):
        a = acc_ref[g]
        o_ref[0, :, g * HEAD_DIM:(g + 1) * HEAD_DIM] = (a[:, :HEAD_DIM] / a[:, HEAD_DIM:]).astype(o_ref.dtype)


def _flash(q, k, v, ctx_len, tk):
    b, l, _ = q.shape
    tq = ROW_TILE
    assert ctx_len == tq and l % tk == 0
    gw = ATT_GROUP * HEAD_DIM
    return pl.pallas_call(
        functools.partial(_flash_kernel, ctx_len=ctx_len, tk=tk),
        out_shape=jax.ShapeDtypeStruct((b, l, ATT_WIDTH), BF16),
        grid=(b, ATT_KV_HEADS, l // tq),
        in_specs=[pl.BlockSpec((1, tq, gw), lambda bi, h, qi: (bi, qi, h)),
                  pl.BlockSpec((1, l, HEAD_DIM), lambda bi, h, qi: (bi, 0, h)),
                  pl.BlockSpec((1, l, 2 * HEAD_DIM), lambda bi, h, qi: (bi, 0, h))],
        out_specs=pl.BlockSpec((1, tq, gw), lambda bi, h, qi: (bi, qi, h)),
        scratch_shapes=[pltpu.VMEM((ATT_GROUP, tq, LANES), F32),
                        pltpu.VMEM((ATT_GROUP, tq, 2 * HEAD_DIM), F32)],
        compiler_params=_cparams(("arbitrary", "arbitrary", "arbitrary")),
        name="flash",
    )(q, k, v)


def _conv3(u, prev_row, next_row, w_ref):
    rows = u.shape[0]
    row = lax.broadcasted_iota(jnp.int32, u.shape, 0)
    u_prev = jnp.where(row == 0, prev_row, pltpu.roll(u, 1, axis=0))
    u_next = jnp.where(row == rows - 1, next_row, pltpu.roll(u, rows - 1, axis=0))
    return u_prev * w_ref[0:1, :] + u * w_ref[1:2, :] + u_next * w_ref[2:3, :]


def _seq_edges(tiles_per_batch):
    tb = pl.program_id(0) % tiles_per_batch
    has_prev = tb >= 2
    has_next = jnp.logical_and(tb >= 1, tb < tiles_per_batch - 1)
    return has_prev, has_next


def _halo_specs(col_block, n_rows):
    rb = ROW_TILE // SUBLANES
    last = n_rows // SUBLANES - 1
    prev = pl.BlockSpec((SUBLANES, CONV_TILE), lambda i, j: (jnp.maximum(i * rb - 1, 0), col_block + j))
    nxt = pl.BlockSpec((SUBLANES, CONV_TILE), lambda i, j: (jnp.minimum((i + 1) * rb, last), col_block + j))
    return prev, nxt


def _sconv_kernel(gb_ref, gc_ref, gx_ref, gcp_ref, gxp_ref, gcn_ref, gxn_ref, w_ref, o_ref, *, tiles_per_batch):
    has_prev, has_next = _seq_edges(tiles_per_batch)
    u = gc_ref[...] * gx_ref[...]
    prev_row = jnp.where(has_prev, gcp_ref[SUBLANES - 1:SUBLANES, :] * gxp_ref[SUBLANES - 1:SUBLANES, :], 0.0)
    next_row = jnp.where(has_next, gcn_ref[0:1, :] * gxn_ref[0:1, :], 0.0)
    o_ref[...] = (gb_ref[...] * _conv3(u, prev_row, next_row, w_ref)).astype(o_ref.dtype)


def _sconv(p, conv_w, tiles_per_batch):
    m = p.shape[0]
    width = conv_w.shape[1]
    base = (ATT_WIDTH + 2 * KV_WIDTH) // CONV_TILE
    nb = width // CONV_TILE
    main = lambda off: pl.BlockSpec((ROW_TILE, CONV_TILE), lambda i, j: (i, off + j))
    gcp, gcn = _halo_specs(base + nb, m)
    gxp, gxn = _halo_specs(base + 2 * nb, m)
    return pl.pallas_call(
        functools.partial(_sconv_kernel, tiles_per_batch=tiles_per_batch),
        out_shape=jax.ShapeDtypeStruct((m, width), BF16),
        grid=(m // ROW_TILE, nb),
        in_specs=[main(base), main(base + nb), main(base + 2 * nb), gcp, gxp, gcn, gxn,
                  pl.BlockSpec((3, CONV_TILE), lambda i, j: (0, j))],
        out_specs=pl.BlockSpec((ROW_TILE, CONV_TILE), lambda i, j: (i, j)),
        compiler_params=_cparams(("arbitrary", "arbitrary")),
        name="sconv",
    )(p, p, p, p, p, p, p, conv_w)


def _ssd_prep_kernel(u_ref, up_ref, un_ref, w_ref, b_ref, o_ref, *, tiles_per_batch, transpose):
    has_prev, has_next = _seq_edges(tiles_per_batch)
    prev_row = jnp.where(has_prev, up_ref[SUBLANES - 1:SUBLANES, :], 0.0)
    next_row = jnp.where(has_next, un_ref[0:1, :], 0.0)
    y = _conv3(u_ref[...], prev_row, next_row, w_ref) + b_ref[...]
    y = y * jax.nn.sigmoid(y)
    for q in range(o_ref.shape[0]):
        yq = y[:, q * GROUP_W:(q + 1) * GROUP_W]
        o_ref[q] = (yq.T if transpose else yq).astype(o_ref.dtype)


def _ssd_prep(p, conv_w, conv_b, p_col, w_col, width, transpose, out_dtype, tiles_per_batch):
    m = p.shape[0]
    ct = PREP_TILE
    per = ct // GROUP_W
    pb, wb, nb = p_col // ct, w_col // ct, width // ct
    rb = ROW_TILE // SUBLANES
    last = m // SUBLANES - 1
    if transpose:
        out_shape = (width // GROUP_W, GROUP_W, m)
        out_spec = pl.BlockSpec((per, GROUP_W, ROW_TILE), lambda i, j: (j, 0, i))
    else:
        out_shape = (width // GROUP_W, m, GROUP_W)
        out_spec = pl.BlockSpec((per, ROW_TILE, GROUP_W), lambda i, j: (j, i, 0))
    return pl.pallas_call(
        functools.partial(_ssd_prep_kernel, tiles_per_batch=tiles_per_batch, transpose=transpose),
        out_shape=jax.ShapeDtypeStruct(out_shape, out_dtype),
        grid=(m // ROW_TILE, nb),
        in_specs=[pl.BlockSpec((ROW_TILE, ct), lambda i, j: (i, pb + j)),
                  pl.BlockSpec((SUBLANES, ct), lambda i, j: (jnp.maximum(i * rb - 1, 0), pb + j)),
                  pl.BlockSpec((SUBLANES, ct), lambda i, j: (jnp.minimum((i + 1) * rb, last), pb + j)),
                  pl.BlockSpec((3, ct), lambda i, j: (0, wb + j)),
                  pl.BlockSpec((1, ct), lambda i, j: (0, wb + j))],
        out_specs=out_spec,
        compiler_params=_cparams(("arbitrary", "arbitrary")),
        name="ssd_prep",
    )(p, p, p, conv_w, conv_b.reshape(1, -1))


def _softplus(x):
    return jnp.maximum(x, 0.0) + jnp.log1p(jnp.exp(-jnp.abs(x)))


def _ssd_scan_kernel(*refs, direction):
    if direction == 0:
        x_ref, b_ref, c_ref, dt_ref, dtt_ref, bias_r_ref, bias_c_ref, alog_r_ref, alog_c_ref, d_ref, \
            y_ref, h_ref, xw_ref = refs
        yf_ref = None
    else:
        x_ref, b_ref, c_ref, dt_ref, dtt_ref, bias_r_ref, bias_c_ref, alog_r_ref, alog_c_ref, yf_ref, \
            y_ref, h_ref, xw_ref = refs
        d_ref = None

    @pl.when(pl.program_id(1) == 0)
    def _():
        h_ref[...] = jnp.zeros(h_ref.shape, F32)

    dt = _softplus(dt_ref[...] + bias_r_ref[...])
    dtt = _softplus(dtt_ref[...] + bias_c_ref[...])
    dta = dt * (-jnp.exp(alog_r_ref[...]))
    dtat = dtt * (-jnp.exp(alog_c_ref[...]))
    row = lax.broadcasted_iota(jnp.int32, (CHUNK, CHUNK), 0)
    col = lax.broadcasted_iota(jnp.int32, (CHUNK, CHUNK), 1)
    mask = (row >= col) if direction == 0 else (row <= col)
    mask_t = (row <= col) if direction == 0 else (row >= col)
    tri = mask.astype(F32)
    a_all = jnp.dot(tri, dta, precision=HIGHEST, preferred_element_type=F32)
    at_all = lax.dot_general(dtat, tri, (((1,), (1,)), ((), ())), precision=HIGHEST,
                             preferred_element_type=F32)
    a_tot = jnp.sum(dtat, axis=1, keepdims=True)
    nt = (((1,), (1,)), ((), ()))

    for g in range(SSM_GROUPS):
        sl = slice((g % 4) * D_STATE, (g % 4 + 1) * D_STATE)
        bg = b_ref[g // 4, :, sl]
        cg = c_ref[g // 4, :, sl]
        cb_t = lax.dot_general(bg, cg, nt, preferred_element_type=F32)
        y_off = lax.dot_general(h_ref[g].astype(BF16), cg, nt, preferred_element_type=F32)
        for k in range(HPG):
            hd = direction * SSM_GROUPS * HPG + g * HPG + k
            ch = slice(k * SSM_HEAD_DIM, (k + 1) * SSM_HEAD_DIM)
            a_col = a_all[:, hd:hd + 1]
            a_row = at_all[hd:hd + 1, :]
            a_last = a_tot[hd:hd + 1, :]
            decay_t = jnp.exp(jnp.where(mask_t, a_row - a_col, NEG_INF))
            xk = x_ref[g, ch, :]
            xdt = xk * dtt[hd:hd + 1, :]
            y = jnp.dot(xdt.astype(BF16), (cb_t * decay_t).astype(BF16), preferred_element_type=F32)
            y = y + y_off[ch, :] * jnp.exp(a_row)
            if direction == 0:
                y = y + d_ref[hd:hd + 1, :] * xk
            else:
                y = y + yf_ref[g, ch, :]
            y_ref[g, ch, :] = y
            xw_ref[ch, :] = (xdt * jnp.exp(a_last - a_row)).astype(BF16)
            h_ref[g, ch, :] = h_ref[g, ch, :] * jnp.exp(a_last)
        h_ref[g] = h_ref[g] + jnp.dot(xw_ref[...], bg, preferred_element_type=F32)


def _ssd_scan(x_t, bc, dt_raw, dtt_raw, dt_bias, a_log, extra, direction, batch):
    _, _, m = x_t.shape
    chunks = m // CHUNK // batch
    ctx_chunks = ROW_TILE // CHUNK

    def rb(bi, t):
        if direction == 0:
            c = t
        else:
            c = jnp.where(t < ctx_chunks, ctx_chunks - 1 - t, chunks - 1 + ctx_chunks - t)
        return bi * chunks + c

    nh = dt_raw.shape[1]
    rowv = lambda v: v.reshape(1, nh)
    colv = lambda v: v.reshape(nh, 1)
    xspec = pl.BlockSpec((SSM_GROUPS, GROUP_W, CHUNK), lambda bi, t: (0, 0, rb(bi, t)))
    const = lambda shape: pl.BlockSpec(shape, lambda bi, t: (0, 0))
    extra_spec = const((nh, 1)) if direction == 0 else xspec
    return pl.pallas_call(
        functools.partial(_ssd_scan_kernel, direction=direction),
        out_shape=jax.ShapeDtypeStruct(x_t.shape, F32),
        grid=(batch, chunks),
        in_specs=[xspec,
                  pl.BlockSpec((2, CHUNK, GROUP_W), lambda bi, t: (2 * direction, rb(bi, t), 0)),
                  pl.BlockSpec((2, CHUNK, GROUP_W), lambda bi, t: (2 * direction + 1, rb(bi, t), 0)),
                  pl.BlockSpec((CHUNK, nh), lambda bi, t: (rb(bi, t), 0)),
                  pl.BlockSpec((nh, CHUNK), lambda bi, t: (0, rb(bi, t))),
                  const((1, nh)), const((nh, 1)), const((1, nh)), const((nh, 1)), extra_spec],
        out_specs=xspec,
        scratch_shapes=[pltpu.VMEM((SSM_GROUPS, GROUP_W, D_STATE), F32),
                        pltpu.VMEM((GROUP_W, CHUNK), BF16)],
        compiler_params=_cparams(("arbitrary", "arbitrary")),
        name=f"ssd_scan{direction}",
    )(x_t, bc, bc, dt_raw, dtt_raw, rowv(dt_bias), colv(dt_bias), rowv(a_log), colv(a_log), extra)


def _ssd_finish_kernel(y_ref, z_ref, nw_ref, o_ref):
    for g in range(SSM_GROUPS):
        cols = slice(g * GROUP_W, (g + 1) * GROUP_W)
        z = z_ref[:, cols]
        y = y_ref[g].T * (z * jax.nn.sigmoid(z))
        y = y * lax.rsqrt(jnp.mean(y * y, axis=-1, keepdims=True) + EPS)
        o_ref[:, cols] = (y * nw_ref[:, cols]).astype(o_ref.dtype)


def _ssd_finish(y_t, p, norm_w):
    _, _, m = y_t.shape
    d_inner = SSM_GROUPS * GROUP_W
    return pl.pallas_call(
        _ssd_finish_kernel,
        out_shape=jax.ShapeDtypeStruct((m, d_inner), BF16),
        grid=(m // ROW_TILE,),
        in_specs=[pl.BlockSpec((SSM_GROUPS, GROUP_W, ROW_TILE), lambda i: (0, 0, i)),
                  pl.BlockSpec((ROW_TILE, d_inner), lambda i: (i, 0)),
                  pl.BlockSpec((1, d_inner), lambda i: (0, 0))],
        out_specs=pl.BlockSpec((ROW_TILE, d_inner), lambda i: (i, 0)),
        compiler_params=_cparams(("arbitrary",)),
        name="ssd_finish",
    )(y_t, p, norm_w.reshape(1, d_inner))


def _norm_router_kernel(h_ref, g_ref, m_ref, w_ref, b_ref, f_ref, o_ref, cnt_ref, carry_ref, whi_ref, wlo_ref,
                        *, shift_row, scale_row):
    @pl.when(pl.program_id(0) == 0)
    def _():
        carry_ref[...] = jnp.zeros(carry_ref.shape, F32)
        w = w_ref[...]
        w_hi = w.astype(BF16)
        whi_ref[...] = w_hi
        wlo_ref[...] = (w - w_hi.astype(F32)).astype(BF16)

    x = h_ref[...]
    y = x * lax.rsqrt(jnp.mean(x * x, axis=-1, keepdims=True) + EPS) * g_ref[...]
    f = y * (1.0 + m_ref[0, scale_row:scale_row + 1, :]) + m_ref[0, shift_row:shift_row + 1, :]
    f_ref[...] = f

    f_hi = f.astype(BF16)
    f_lo = (f - f_hi.astype(F32)).astype(BF16)
    logits = (jnp.dot(f_hi, whi_ref[...], preferred_element_type=F32)
              + jnp.dot(f_lo, whi_ref[...], preferred_element_type=F32)
              + jnp.dot(f_hi, wlo_ref[...], preferred_element_type=F32)) + b_ref[...]
    rows = logits.shape[0]
    lane = lax.broadcasted_iota(jnp.int32, logits.shape, 1)
    big = jnp.int32(1 << 20)

    def first_lane(cond):
        return jnp.min(jnp.where(cond, lane, big), axis=-1, keepdims=True)

    gmask = lane < N_GROUPS
    gl = jnp.where(gmask, logits, NEG_INF)
    ge = jnp.exp(gl - jnp.max(gl, axis=-1, keepdims=True))
    g_prob = ge / jnp.sum(ge, axis=-1, keepdims=True)
    g_p = jnp.max(g_prob, axis=-1, keepdims=True)
    g_sel = first_lane(jnp.logical_and(gmask, g_prob == g_p))
    lo = N_GROUPS + EXP_PER_GROUP * g_sel
    emask = jnp.logical_and(lane >= lo, lane < lo + EXP_PER_GROUP)
    el = jnp.where(emask, logits, NEG_INF)
    ee = jnp.exp(el - jnp.max(el, axis=-1, keepdims=True))
    e_prob = jnp.where(emask, ee / jnp.sum(ee, axis=-1, keepdims=True), -1.0)
    p1 = jnp.max(e_prob, axis=-1, keepdims=True)
    i1 = first_lane(e_prob == p1)
    e_rest = jnp.where(lane == i1, -1.0, e_prob)
    p2 = jnp.max(e_rest, axis=-1, keepdims=True)
    i2 = first_lane(e_rest == p2)
    denom = p1 + p2
    gate1 = g_p * p1 / denom
    gate2 = g_p * p2 / denom
    e1 = i1 - N_GROUPS
    e2 = i2 - N_GROUPS

    oh1 = lane == e1
    oh2 = lane == e2
    ohs = jnp.logical_or(oh1, oh2).astype(F32)
    r_i = lax.broadcasted_iota(jnp.int32, (rows, rows), 0)
    c_i = lax.broadcasted_iota(jnp.int32, (rows, rows), 1)
    before = (r_i > c_i).astype(F32).astype(BF16)
    seen = jnp.dot(before, ohs.astype(BF16), preferred_element_type=F32) + carry_ref[...]
    rank1 = jnp.sum(jnp.where(oh1, seen, 0.0), axis=-1, keepdims=True)
    rank2 = jnp.sum(jnp.where(oh2, seen, 0.0), axis=-1, keepdims=True)
    carry_ref[...] = carry_ref[...] + jnp.sum(ohs, axis=0, keepdims=True)

    slab = jnp.zeros(logits.shape, F32)
    for idx, val in enumerate((e1.astype(F32), e2.astype(F32), gate1, gate2, rank1, rank2)):
        slab = jnp.where(lane == idx, val, slab)
    o_ref[...] = slab
    cnt_ref[...] = jnp.broadcast_to(carry_ref[...], cnt_ref.shape)


def _norm_router(h, g, modt, shift_row, scale_row, w, b, tpb, batch):
    t, d = h.shape
    return pl.pallas_call(
        functools.partial(_norm_router_kernel, shift_row=shift_row, scale_row=scale_row),
        out_shape=(jax.ShapeDtypeStruct((t, d), F32),
                   jax.ShapeDtypeStruct((t, LANES), F32),
                   jax.ShapeDtypeStruct((SUBLANES, LANES), F32)),
        grid=(t // ROW_TILE,),
        in_specs=[pl.BlockSpec((ROW_TILE, d), lambda i: (i, 0)),
                  pl.BlockSpec((1, d), lambda i: (0, 0)),
                  pl.BlockSpec((1, SUBLANES, d), lambda i: (_cond_of_tile(i, tpb, batch), 0, 0)),
                  pl.BlockSpec((d, LANES), lambda i: (0, 0)),
                  pl.BlockSpec((1, LANES), lambda i: (0, 0))],
        out_specs=(pl.BlockSpec((ROW_TILE, d), lambda i: (i, 0)),
                   pl.BlockSpec((ROW_TILE, LANES), lambda i: (i, 0)),
                   pl.BlockSpec((SUBLANES, LANES), lambda i: (0, 0))),
        scratch_shapes=[pltpu.VMEM((1, LANES), F32), pltpu.VMEM((d, LANES), BF16), pltpu.VMEM((d, LANES), BF16)],
        compiler_params=_cparams(("arbitrary",)),
        name="norm_router",
    )(h, g.reshape(1, d), modt, w, b)


def _row_copy(src_hbm, dst_ref, sem, src_row, dst_row):
    return pltpu.make_async_copy(src_hbm.at[src_row], dst_ref.at[pl.ds(dst_row, 1)], sem)


def _gather_rows_loop(idx_ref, k, src_hbm, dst_ref, sem, n_rows):
    def body(r, carry):
        _row_copy(src_hbm, dst_ref, sem, idx_ref[0, k, r], r).start()
        return carry

    lax.fori_loop(0, n_rows, body, 0)


def _gather_rows_unrolled(idx_ref, k, src_hbm, dst_ref, sem, n_rows):
    for r in range(n_rows):
        _row_copy(src_hbm, dst_ref, sem, idx_ref[0, k, r], r).start()


def _wait_rows(dst_ref, sem):
    pltpu.make_async_copy(dst_ref, dst_ref, sem).wait()


def _expert_kernel(blk_e_ref, cur_ref, nxt_ref, f_hbm, wg_ref, wu_ref, wd_ref, o_ref, xbuf, xb_ref, sems):
    i = pl.program_id(0)
    slot = i % 2

    @pl.when(i == 0)
    def _():
        _gather_rows_loop(cur_ref, 0, f_hbm, xbuf.at[0], sems.at[0], MOE_BLOCK)

    _wait_rows(xbuf.at[slot], sems.at[slot])
    xb_ref[...] = xbuf[slot].astype(BF16)
    _gather_rows_unrolled(nxt_ref, 0, f_hbm, xbuf.at[1 - slot], sems.at[1 - slot], MOE_BLOCK)
    x = xb_ref[...]
    hg = jnp.dot(x, wg_ref[0, 0], preferred_element_type=F32)
    hu = jnp.dot(x, wu_ref[0, 0], preferred_element_type=F32)
    act = (hg * jax.nn.sigmoid(hg) * hu).astype(BF16)
    o_ref[...] = jnp.dot(act, wd_ref[0, 0], preferred_element_type=F32)

    @pl.when(i == pl.num_programs(0) - 1)
    def _():
        _wait_rows(xbuf.at[1 - slot], sems.at[1 - slot])


def _experts(f, slot_t, blk_e, wg, wu, wd, layer):
    t, _, d = f.shape
    n_blocks = blk_e.shape[0]
    de = wg.shape[3]
    slots3 = slot_t.reshape(n_blocks, 1, MOE_BLOCK)
    grid_spec = pltpu.PrefetchScalarGridSpec(
        num_scalar_prefetch=1,
        grid=(n_blocks,),
        in_specs=[pl.BlockSpec((1, 1, MOE_BLOCK), lambda i, be: (i, 0, 0), memory_space=pltpu.SMEM),
                  pl.BlockSpec((1, 1, MOE_BLOCK), lambda i, be: (jnp.minimum(i + 1, n_blocks - 1), 0, 0),
                               memory_space=pltpu.SMEM),
                  pl.BlockSpec(memory_space=pl.ANY),
                  pl.BlockSpec((1, 1, d, de), lambda i, be: (layer, be[i], 0, 0)),
                  pl.BlockSpec((1, 1, d, de), lambda i, be: (layer, be[i], 0, 0)),
                  pl.BlockSpec((1, 1, de, d), lambda i, be: (layer, be[i], 0, 0))],
        out_specs=pl.BlockSpec((MOE_BLOCK, d), lambda i, be: (i, 0)),
        scratch_shapes=[pltpu.VMEM((2, MOE_BLOCK, d), F32), pltpu.VMEM((MOE_BLOCK, d), BF16),
                        pltpu.SemaphoreType.DMA((2,))],
    )
    return pl.pallas_call(
        _expert_kernel,
        out_shape=jax.ShapeDtypeStruct((n_blocks * MOE_BLOCK, d), F32),
        grid_spec=grid_spec,
        compiler_params=_cparams(("arbitrary",)),
        name="experts",
    )(blk_e, slots3, slots3, f, wg, wu, wd)


def _combine_kernel(cur_ref, nxt_ref, y_hbm, h_ref, slab_ref, m_ref, o_ref, ybuf, sems, *, gate_row):
    i = pl.program_id(0)
    slot = i % 2

    @pl.when(i == 0)
    def _():
        for k in range(2):
            _gather_rows_loop(cur_ref, k, y_hbm, ybuf.at[0, k], sems.at[0, k], ROW_TILE)

    for k in range(2):
        _wait_rows(ybuf.at[slot, k], sems.at[slot, k])
    for k in range(2):
        _gather_rows_unrolled(nxt_ref, k, y_hbm, ybuf.at[1 - slot, k], sems.at[1 - slot, k], ROW_TILE)
    slab = slab_ref[...]
    y = slab[:, 2:3] * ybuf[slot, 0] + slab[:, 3:4] * ybuf[slot, 1]
    o_ref[...] = h_ref[...] + m_ref[0, gate_row:gate_row + 1, :] * y

    @pl.when(i == pl.num_programs(0) - 1)
    def _():
        for k in range(2):
            _wait_rows(ybuf.at[1 - slot, k], sems.at[1 - slot, k])


def _combine(y_sorted, dest3, h, slab, modt, gate_row, tpb, batch):
    t, d = h.shape
    nt = t // ROW_TILE
    return pl.pallas_call(
        functools.partial(_combine_kernel, gate_row=gate_row),
        out_shape=jax.ShapeDtypeStruct((t, d), F32),
        grid=(nt,),
        in_specs=[pl.BlockSpec((1, 2, ROW_TILE), lambda i: (i, 0, 0), memory_space=pltpu.SMEM),
                  pl.BlockSpec((1, 2, ROW_TILE), lambda i: (jnp.minimum(i + 1, nt - 1), 0, 0),
                               memory_space=pltpu.SMEM),
                  pl.BlockSpec(memory_space=pl.ANY),
                  pl.BlockSpec((ROW_TILE, d), lambda i: (i, 0)),
                  pl.BlockSpec((ROW_TILE, LANES), lambda i: (i, 0)),
                  pl.BlockSpec((1, SUBLANES, d), lambda i: (_cond_of_tile(i, tpb, batch), 0, 0))],
        out_specs=pl.BlockSpec((ROW_TILE, d), lambda i: (i, 0)),
        scratch_shapes=[pltpu.VMEM((2, 2, ROW_TILE, d), F32), pltpu.SemaphoreType.DMA((2, 2))],
        compiler_params=_cparams(("arbitrary",)),
        name="moe_combine",
    )(dest3, dest3, y_sorted, h, slab, modt)


def _moe(h, g, modt, w_grp, b_grp, w_exp, b_exp, wg, wu, wd, layer, tpb, batch):
    t, d = h.shape
    n_r = N_GROUPS + N_EXPERTS
    w_r = jnp.zeros((d, LANES), F32).at[:, :n_r].set(jnp.concatenate([w_grp, w_exp], axis=1))
    b_r = jnp.zeros((1, LANES), F32).at[0, :n_r].set(jnp.concatenate([b_grp, b_exp]))
    f, slab, cnt = _norm_router(h, g, modt, 3, 4, w_r, b_r, tpb, batch)

    counts = cnt[0, :N_EXPERTS].astype(jnp.int32)
    pcounts = (counts + MOE_BLOCK - 1) // MOE_BLOCK * MOE_BLOCK
    pends = jnp.cumsum(pcounts)
    pstarts = pends - pcounts
    eid = slab[:, 0:2].astype(jnp.int32)
    rank = slab[:, 4:6].astype(jnp.int32)
    dest = pstarts[eid] + rank
    n_blocks = -(-2 * t // MOE_BLOCK) + N_EXPERTS
    n_slots = n_blocks * MOE_BLOCK
    tok = jnp.broadcast_to(jnp.arange(t, dtype=jnp.int32)[:, None], (t, 2))
    slot_t = jnp.zeros((n_slots,), jnp.int32).at[dest.reshape(-1)].set(tok.reshape(-1))
    blk_start = jnp.arange(n_blocks, dtype=jnp.int32) * MOE_BLOCK
    blk_e = jnp.minimum(jnp.sum(pends[None, :] <= blk_start[:, None], axis=1), N_EXPERTS - 1).astype(jnp.int32)

    y_sorted = _experts(f.reshape(t, 1, d), slot_t, blk_e, wg, wu, wd, layer)
    dest3 = dest.reshape(t // ROW_TILE, ROW_TILE, 2).transpose(0, 2, 1)
    return _combine(y_sorted.reshape(n_slots, 1, d), dest3, h, slab, modt, 5, tpb, batch)


def _rope_tables(ctx_len, seq):
    n_rows = seq // GRID_W
    rowp = jnp.repeat(jnp.arange(n_rows, dtype=F32), GRID_W)
    colp = jnp.tile(jnp.arange(GRID_W, dtype=F32), n_rows)
    axis_dim = HEAD_DIM // 2
    inv = ROPE_THETA ** (-jnp.arange(0, axis_dim, 2, dtype=F32) / axis_dim)
    ar = rowp[:, None] * inv
    ac = colp[:, None] * inv
    cos = jnp.concatenate([jnp.cos(ar), jnp.cos(ar), jnp.cos(ac), jnp.cos(ac)], axis=-1)
    sin = jnp.concatenate([-jnp.sin(ar), jnp.sin(ar), -jnp.sin(ac), jnp.sin(ac)], axis=-1)
    cos = jnp.concatenate([jnp.ones((ctx_len, HEAD_DIM), F32), cos], axis=0)
    sin = jnp.concatenate([jnp.zeros((ctx_len, HEAD_DIM), F32), sin], axis=0)
    return cos, sin


def _pick_tile(n, candidates):
    for c in candidates:
        if n % c == 0:
            return c
    raise ValueError(f"no tile for {n}")


def kernel(x, c, ctx, c_ctx, ada_w, ada_b, norm_mix_g, norm_ffn_g, att_w_in, att_w_out, att_q_g, att_k_g, sconv_w, ssm_w_in, ssm_conv_w, ssm_conv_b, ssm_dt_bias, ssm_a_log, ssm_d, ssm_norm_w, ssm_w_out, router_grp_w, router_grp_b, router_exp_w, router_exp_b, exp_w_gate, exp_w_up, exp_w_down):
    b, s, d = x.shape
    cl = ctx.shape[1]
    assert cl == ROW_TILE and s % ROW_TILE == 0 and s % GRID_W == 0
    depth = ada_w.shape[0]
    l = cl + s
    m = b * l
    tpb = l // ROW_TILE
    d_inner = SSM_GROUPS * GROUP_W
    conv_dim = ssm_conv_w.shape[2]

    cond = jnp.concatenate([c, c_ctx[None]], axis=0)
    cond_x = jnp.broadcast_to(cond[:, :, None], (b + 1, d, LANES))
    mods = _ada_all(cond_x, ada_w, ada_b).reshape(depth, SUBLANES, 6, d)
    modt = jnp.pad(mods, ((0, 0), (0, 0), (0, SUBLANES - 6), (0, 0)))

    cos_t, sin_t = _rope_tables(cl, s)
    tm = _pick_tile(m, (512, 256))
    tk = _pick_tile(l, (1408, 768, 512, 256))

    wg, wu, wd = exp_w_gate.astype(BF16), exp_w_up.astype(BF16), exp_w_down.astype(BF16)
    w_att_out, w_ssm_out = att_w_out.astype(BF16), ssm_w_out.astype(BF16)
    h = jnp.concatenate([ctx, x], axis=1).reshape(m, d)
    for i in range(depth):
        j = i // 2
        n = _norm_mod(h, norm_mix_g[i], modt[i], 0, 1, BF16, tpb, b)
        if i % 2 == 0:
            p = _mm(n, att_w_in, j, 0, att_w_in.shape[2], F32, tm, 1536)
            q, k, v = _qkv_prep(p, cos_t, sin_t, att_q_g[j], att_k_g[j], tpb)
            o = _flash(q.reshape(b, l, ATT_WIDTH), k.reshape(b, l, KV_WIDTH), v.reshape(b, l, 2 * KV_WIDTH), cl, tk)
            cv = _sconv(p, sconv_w[j], tpb)
            y = jnp.concatenate([o.reshape(m, ATT_WIDTH), cv], axis=-1)
            h = _mm_res(y, w_att_out, j, h, modt[i], 2, 1024, tpb, b)
        else:
            zx_w = d_inner + conv_dim
            p = _mm(n, ssm_w_in, j, 0, zx_w, F32, tm, 1536)
            dt_raw = _mm(n, ssm_w_in, j, zx_w, ssm_w_in.shape[2] - zx_w, F32, tm, LANES)
            x_t = _ssd_prep(p, ssm_conv_w[j], ssm_conv_b[j], d_inner, 0, d_inner, True, F32, tpb)
            bc = _ssd_prep(p, ssm_conv_w[j], ssm_conv_b[j], 2 * d_inner, d_inner, conv_dim - d_inner, False, BF16, tpb)
            dtt_raw = dt_raw.T
            bias = ssm_dt_bias[j].reshape(-1)
            alog = ssm_a_log[j].reshape(-1)
            d_col = jnp.tile(ssm_d[j], 2).reshape(-1, 1)
            yf = _ssd_scan(x_t, bc, dt_raw, dtt_raw, bias, alog, d_col, 0, b)
            y_t = _ssd_scan(x_t, bc, dt_raw, dtt_raw, bias, alog, yf, 1, b)
            y = _ssd_finish(y_t, p, ssm_norm_w[j])
            h = _mm_res(y, w_ssm_out, j, h, modt[i], 2, 1024, tpb, b)
        h = _moe(h, norm_ffn_g[i], modt[i], router_grp_w[i], router_grp_b[i], router_exp_w[i], router_exp_b[i],
                 wg, wu, wd, i, tpb, b)
    return h.reshape(b, l, d)[:, cl:, :]
```

```python
import functools

import jax
import jax.numpy as jnp
from jax import lax
from jax.experimental import pallas as pl
from jax.experimental.pallas import tpu as pltpu

F32 = jnp.float32
BF16 = jnp.bfloat16
HIGHEST = lax.Precision.HIGHEST

EPS = 1e-6
LANES = 128
SUBLANES = 8
ROW_TILE = 256
GRID_W = 64
ROPE_THETA = 10000.0

ATT_HEADS = 8
ATT_KV_HEADS = 2
ATT_GROUP = ATT_HEADS // ATT_KV_HEADS
HEAD_DIM = 128
ATT_WIDTH = ATT_HEADS * HEAD_DIM
KV_WIDTH = ATT_KV_HEADS * HEAD_DIM
CONV_TILE = 512
PREP_TILE = 2048

SSM_HEAD_DIM = 64
SSM_GROUPS = 8
HPG = 8
D_STATE = 128
CHUNK = 128
GROUP_W = HPG * SSM_HEAD_DIM

N_GROUPS = 4
EXP_PER_GROUP = 8
N_EXPERTS = N_GROUPS * EXP_PER_GROUP
MOE_BLOCK = 256
TOK_PITCH = 24

NEG_INF = float("-inf")
LOG2_E = 1.4426950408889634


def _cparams(sem, vmem_mb=None):
    kw = dict(dimension_semantics=sem)
    if vmem_mb is not None:
        kw["vmem_limit_bytes"] = vmem_mb * 1024 * 1024
    return pltpu.CompilerParams(**kw)


def _lane_tile(x, n):
    return jnp.concatenate([x] * n, axis=-1)


def _ada_kernel(cx_ref, w_ref, b_ref, o_ref, s_ref, *, n_cond):
    @pl.when(jnp.logical_and(pl.program_id(0) == 0, pl.program_id(1) == 0))
    def _():
        cx = cx_ref[...]
        s_ref[...] = cx * jax.nn.sigmoid(cx)

    d = cx_ref.shape[1]
    tn = o_ref.shape[2]

    def body(i, accs):
        k0 = pl.multiple_of(i * SUBLANES, SUBLANES)
        w = w_ref[0, pl.ds(k0, SUBLANES), :]
        return tuple(acc + w * _lane_tile(s_ref[r, pl.ds(k0, SUBLANES), :], tn // LANES)
                     for r, acc in enumerate(accs))

    accs = lax.fori_loop(0, d // SUBLANES, body,
                         tuple(jnp.zeros((SUBLANES, tn), F32) for _ in range(n_cond)), unroll=4)
    rows = [jnp.sum(acc, axis=0, keepdims=True) + b_ref[0] for acc in accs]
    rows += [jnp.zeros((1, tn), F32)] * (SUBLANES - n_cond)
    o_ref[0] = jnp.concatenate(rows, axis=0)


def _ada_all(cond_x, ada_w, ada_b):
    depth, d, n = ada_w.shape
    n_cond = cond_x.shape[0]
    tn = 1024
    return pl.pallas_call(
        functools.partial(_ada_kernel, n_cond=n_cond),
        out_shape=jax.ShapeDtypeStruct((depth, SUBLANES, n), F32),
        grid=(depth, n // tn),
        in_specs=[pl.BlockSpec((n_cond, d, LANES), lambda l, j: (0, 0, 0)),
                  pl.BlockSpec((1, d, tn), lambda l, j: (l, 0, j)),
                  pl.BlockSpec((1, 1, tn), lambda l, j: (l, 0, j))],
        out_specs=pl.BlockSpec((1, SUBLANES, tn), lambda l, j: (l, 0, j)),
        scratch_shapes=[pltpu.VMEM((n_cond, d, LANES), F32)],
        compiler_params=_cparams(("arbitrary", "arbitrary")),
        name="ada_mod",
    )(cond_x, ada_w, ada_b.reshape(depth, 1, n))


def _norm_mod_kernel(h_ref, g_ref, m_ref, o_ref, *, shift_row, scale_row):
    x = h_ref[...]
    y = x * lax.rsqrt(jnp.mean(x * x, axis=-1, keepdims=True) + EPS) * g_ref[...]
    sh = m_ref[0, shift_row:shift_row + 1, :]
    sc = m_ref[0, scale_row:scale_row + 1, :]
    o_ref[...] = (y * (1.0 + sc) + sh).astype(o_ref.dtype)


def _cond_of_tile(i, tiles_per_batch, batch):
    return jnp.where(i % tiles_per_batch == 0, batch, i // tiles_per_batch)


def _norm_mod(h, g, modt, shift_row, scale_row, out_dtype, tpb, batch):
    m, d = h.shape
    return pl.pallas_call(
        functools.partial(_norm_mod_kernel, shift_row=shift_row, scale_row=scale_row),
        out_shape=jax.ShapeDtypeStruct((m, d), out_dtype),
        grid=(m // ROW_TILE,),
        in_specs=[pl.BlockSpec((ROW_TILE, d), lambda i: (i, 0)),
                  pl.BlockSpec((1, d), lambda i: (0, 0)),
                  pl.BlockSpec((1, SUBLANES, d), lambda i: (_cond_of_tile(i, tpb, batch), 0, 0))],
        out_specs=pl.BlockSpec((ROW_TILE, d), lambda i: (i, 0)),
        compiler_params=_cparams(("arbitrary",)),
        name="norm_mod",
    )(h, g.reshape(1, d), modt)


def _mm_kernel(x_ref, w_ref, o_ref, wb_ref):
    @pl.when(pl.program_id(1) == 0)
    def _():
        wb_ref[...] = w_ref[0].astype(BF16)

    o_ref[...] = jnp.dot(x_ref[...], wb_ref[...], preferred_element_type=F32).astype(o_ref.dtype)


def _mm(x, w3, layer, col0, n, out_dtype, tm, tn):
    m, k = x.shape
    cb = col0 // tn
    return pl.pallas_call(
        _mm_kernel,
        out_shape=jax.ShapeDtypeStruct((m, n), out_dtype),
        grid=(n // tn, m // tm),
        in_specs=[pl.BlockSpec((tm, k), lambda j, i: (i, 0)),
                  pl.BlockSpec((1, k, tn), lambda j, i: (layer, 0, cb + j))],
        out_specs=pl.BlockSpec((tm, tn), lambda j, i: (i, j)),
        scratch_shapes=[pltpu.VMEM((k, tn), BF16)],
        compiler_params=_cparams(("arbitrary", "arbitrary")),
        name="mm",
    )(x, w3)


def _mm_res_kernel(x_ref, w_ref, r_ref, m_ref, o_ref, *, gate_row):
    acc = jnp.dot(x_ref[...], w_ref[0], preferred_element_type=F32)
    o_ref[...] = r_ref[...] + m_ref[0, gate_row:gate_row + 1, :] * acc


def _mm_res(x, w3, layer, res, modt, gate_row, tn, tpb, batch):
    m, k = x.shape
    n = w3.shape[2]
    return pl.pallas_call(
        functools.partial(_mm_res_kernel, gate_row=gate_row),
        out_shape=jax.ShapeDtypeStruct((m, n), F32),
        grid=(n // tn, m // ROW_TILE),
        in_specs=[pl.BlockSpec((ROW_TILE, k), lambda j, i: (i, 0)),
                  pl.BlockSpec((1, k, tn), lambda j, i: (layer, 0, j)),
                  pl.BlockSpec((ROW_TILE, tn), lambda j, i: (i, j)),
                  pl.BlockSpec((1, SUBLANES, tn), lambda j, i: (_cond_of_tile(i, tpb, batch), 0, j))],
        out_specs=pl.BlockSpec((ROW_TILE, tn), lambda j, i: (i, j)),
        compiler_params=_cparams(("arbitrary", "arbitrary")),
        name="mm_res",
    )(x, w3, res, modt)


def _qkv_prep_kernel(p_ref, cos_ref, sin_ref, qg_ref, kg_ref, q_ref, k_ref, v_ref):
    cos = cos_ref[...]
    sin = sin_ref[...]
    lane = lax.broadcasted_iota(jnp.int32, cos.shape, 1)
    first_half = (lane % 64) < 32

    def norm_rope(x, g):
        y = x * lax.rsqrt(jnp.mean(x * x, axis=-1, keepdims=True) + EPS) * g
        partner = jnp.where(first_half, pltpu.roll(y, 96, axis=1), pltpu.roll(y, 32, axis=1))
        return y * cos + partner * sin

    scale = HEAD_DIM ** -0.5 * LOG2_E
    for h in range(ATT_HEADS):
        x = p_ref[:, h * HEAD_DIM:(h + 1) * HEAD_DIM]
        q_ref[:, h * HEAD_DIM:(h + 1) * HEAD_DIM] = (norm_rope(x, qg_ref[...]) * scale).astype(q_ref.dtype)
    ones = jnp.ones((p_ref.shape[0], HEAD_DIM), v_ref.dtype)
    for h in range(ATT_KV_HEADS):
        c0 = ATT_WIDTH + h * HEAD_DIM
        k_ref[:, h * HEAD_DIM:(h + 1) * HEAD_DIM] = norm_rope(p_ref[:, c0:c0 + HEAD_DIM], kg_ref[...]).astype(k_ref.dtype)
        v0 = ATT_WIDTH + KV_WIDTH + h * HEAD_DIM
        v_ref[:, 2 * h * HEAD_DIM:(2 * h + 1) * HEAD_DIM] = p_ref[:, v0:v0 + HEAD_DIM].astype(v_ref.dtype)
        v_ref[:, (2 * h + 1) * HEAD_DIM:(2 * h + 2) * HEAD_DIM] = ones


def _qkv_prep(p, cos_t, sin_t, q_g, k_g, tiles_per_batch):
    m = p.shape[0]
    w = ATT_WIDTH + 2 * KV_WIDTH
    return pl.pallas_call(
        _qkv_prep_kernel,
        out_shape=(jax.ShapeDtypeStruct((m, ATT_WIDTH), BF16),
                   jax.ShapeDtypeStruct((m, KV_WIDTH), BF16),
                   jax.ShapeDtypeStruct((m, 2 * KV_WIDTH), BF16)),
        grid=(m // ROW_TILE,),
        in_specs=[pl.BlockSpec((ROW_TILE, w), lambda i: (i, 0)),
                  pl.BlockSpec((ROW_TILE, HEAD_DIM), lambda i: (i % tiles_per_batch, 0)),
                  pl.BlockSpec((ROW_TILE, HEAD_DIM), lambda i: (i % tiles_per_batch, 0)),
                  pl.BlockSpec((1, HEAD_DIM), lambda i: (0, 0)),
                  pl.BlockSpec((1, HEAD_DIM), lambda i: (0, 0))],
        out_specs=(pl.BlockSpec((ROW_TILE, ATT_WIDTH), lambda i: (i, 0)),
                   pl.BlockSpec((ROW_TILE, KV_WIDTH), lambda i: (i, 0)),
                   pl.BlockSpec((ROW_TILE, 2 * KV_WIDTH), lambda i: (i, 0))),
        compiler_params=_cparams(("arbitrary",)),
        name="qkv_prep",
    )(p, cos_t, sin_t, q_g.reshape(1, HEAD_DIM), k_g.reshape(1, HEAD_DIM))


def _flash_kernel(q_ref, k_ref, v_ref, o_ref, m_ref, acc_ref, *, ctx_len, tk):
    qi = pl.program_id(2)
    n_keys = k_ref.shape[1]

    m_ref[...] = jnp.full(m_ref.shape, NEG_INF, F32)
    acc_ref[...] = jnp.zeros(acc_ref.shape, F32)

    def block(k, v):
        width = k.shape[0]
        for g in range(ATT_GROUP):
            q = q_ref[0, :, g * HEAD_DIM:(g + 1) * HEAD_DIM]
            s = lax.dot_general(q, k, (((1,), (1,)), ((), ())), preferred_element_type=F32)
            m_prev = m_ref[g]
            m_new = jnp.maximum(m_prev, jnp.max(s, axis=-1, keepdims=True))
            alpha = jnp.exp2(m_prev - m_new)
            p = jnp.exp2(s - _lane_tile(m_new, width // LANES))
            pv = jnp.dot(p.astype(BF16), v, preferred_element_type=F32)
            acc_ref[g] = _lane_tile(alpha, 2) * acc_ref[g] + pv
            m_ref[g] = m_new

    @pl.when(qi == 0)
    def _():
        block(k_ref[0, :ctx_len, :], v_ref[0, :ctx_len, :])

    @pl.when(qi > 0)
    def _():
        def body(j, carry):
            r0 = pl.multiple_of(j * tk, tk)
            block(k_ref[0, pl.ds(r0, tk), :], v_ref[0, pl.ds(r0, tk), :])
            return carry
        lax.fori_loop(0, n_keys // tk, body, 0)

    for g in range(ATT_GROUP):
        a = acc_ref[g]
        o_ref[0, :, g * HEAD_DIM:(g + 1) * HEAD_DIM] = (a[:, :HEAD_DIM] / a[:, HEAD_DIM:]).astype(o_ref.dtype)


def _flash(q, k, v, ctx_len, tk):
    b, l, _ = q.shape
    tq = ROW_TILE
    assert ctx_len == tq and l % tk == 0
    gw = ATT_GROUP * HEAD_DIM
    return pl.pallas_call(
        functools.partial(_flash_kernel, ctx_len=ctx_len, tk=tk),
        out_shape=jax.ShapeDtypeStruct((b, l, ATT_WIDTH), BF16),
        grid=(b, ATT_KV_HEADS, l // tq),
        in_specs=[pl.BlockSpec((1, tq, gw), lambda bi, h, qi: (bi, qi, h)),
                  pl.BlockSpec((1, l, HEAD_DIM), lambda bi, h, qi: (bi, 0, h)),
                  pl.BlockSpec((1, l, 2 * HEAD_DIM), lambda bi, h, qi: (bi, 0, h))],
        out_specs=pl.BlockSpec((1, tq, gw), lambda bi, h, qi: (bi, qi, h)),
        scratch_shapes=[pltpu.VMEM((ATT_GROUP, tq, LANES), F32),
                        pltpu.VMEM((ATT_GROUP, tq, 2 * HEAD_DIM), F32)],
        compiler_params=_cparams(("arbitrary", "arbitrary", "arbitrary")),
        name="flash",
    )(q, k, v)


def _conv3(u, prev_row, next_row, w_ref):
    rows = u.shape[0]
    row = lax.broadcasted_iota(jnp.int32, u.shape, 0)
    u_prev = jnp.where(row == 0, prev_row, pltpu.roll(u, 1, axis=0))
    u_next = jnp.where(row == rows - 1, next_row, pltpu.roll(u, rows - 1, axis=0))
    return u_prev * w_ref[0:1, :] + u * w_ref[1:2, :] + u_next * w_ref[2:3, :]


def _seq_edges(tiles_per_batch):
    tb = pl.program_id(0) % tiles_per_batch
    has_prev = tb >= 2
    has_next = jnp.logical_and(tb >= 1, tb < tiles_per_batch - 1)
    return has_prev, has_next


def _halo_specs(col_block, n_rows):
    rb = ROW_TILE // SUBLANES
    last = n_rows // SUBLANES - 1
    prev = pl.BlockSpec((SUBLANES, CONV_TILE), lambda i, j: (jnp.maximum(i * rb - 1, 0), col_block + j))
    nxt = pl.BlockSpec((SUBLANES, CONV_TILE), lambda i, j: (jnp.minimum((i + 1) * rb, last), col_block + j))
    return prev, nxt


def _sconv_kernel(gb_ref, gc_ref, gx_ref, gcp_ref, gxp_ref, gcn_ref, gxn_ref, w_ref, o_ref, *, tiles_per_batch):
    has_prev, has_next = _seq_edges(tiles_per_batch)
    u = gc_ref[...] * gx_ref[...]
    prev_row = jnp.where(has_prev, gcp_ref[SUBLANES - 1:SUBLANES, :] * gxp_ref[SUBLANES - 1:SUBLANES, :], 0.0)
    next_row = jnp.where(has_next, gcn_ref[0:1, :] * gxn_ref[0:1, :], 0.0)
    o_ref[...] = (gb_ref[...] * _conv3(u, prev_row, next_row, w_ref)).astype(o_ref.dtype)


def _sconv(p, conv_w, tiles_per_batch):
    m = p.shape[0]
    width = conv_w.shape[1]
    base = (ATT_WIDTH + 2 * KV_WIDTH) // CONV_TILE
    nb = width // CONV_TILE
    main = lambda off: pl.BlockSpec((ROW_TILE, CONV_TILE), lambda i, j: (i, off + j))
    gcp, gcn = _halo_specs(base + nb, m)
    gxp, gxn = _halo_specs(base + 2 * nb, m)
    return pl.pallas_call(
        functools.partial(_sconv_kernel, tiles_per_batch=tiles_per_batch),
        out_shape=jax.ShapeDtypeStruct((m, width), BF16),
        grid=(m // ROW_TILE, nb),
        in_specs=[main(base), main(base + nb), main(base + 2 * nb), gcp, gxp, gcn, gxn,
                  pl.BlockSpec((3, CONV_TILE), lambda i, j: (0, j))],
        out_specs=pl.BlockSpec((ROW_TILE, CONV_TILE), lambda i, j: (i, j)),
        compiler_params=_cparams(("arbitrary", "arbitrary")),
        name="sconv",
    )(p, p, p, p, p, p, p, conv_w)


def _ssd_prep_kernel(u_ref, up_ref, un_ref, w_ref, b_ref, o_ref, *, tiles_per_batch, transpose):
    has_prev, has_next = _seq_edges(tiles_per_batch)
    prev_row = jnp.where(has_prev, up_ref[SUBLANES - 1:SUBLANES, :], 0.0)
    next_row = jnp.where(has_next, un_ref[0:1, :], 0.0)
    y = _conv3(u_ref[...], prev_row, next_row, w_ref) + b_ref[...]
    y = y * jax.nn.sigmoid(y)
    for q in range(o_ref.shape[0]):
        yq = y[:, q * GROUP_W:(q + 1) * GROUP_W]
        o_ref[q] = (yq.T if transpose else yq).astype(o_ref.dtype)


def _ssd_prep(p, conv_w, conv_b, p_col, w_col, width, transpose, out_dtype, tiles_per_batch):
    m = p.shape[0]
    ct = PREP_TILE
    per = ct // GROUP_W
    pb, wb, nb = p_col // ct, w_col // ct, width // ct
    rb = ROW_TILE // SUBLANES
    last = m // SUBLANES - 1
    if transpose:
        out_shape = (width // GROUP_W, GROUP_W, m)
        out_spec = pl.BlockSpec((per, GROUP_W, ROW_TILE), lambda i, j: (j, 0, i))
    else:
        out_shape = (width // GROUP_W, m, GROUP_W)
        out_spec = pl.BlockSpec((per, ROW_TILE, GROUP_W), lambda i, j: (j, i, 0))
    return pl.pallas_call(
        functools.partial(_ssd_prep_kernel, tiles_per_batch=tiles_per_batch, transpose=transpose),
        out_shape=jax.ShapeDtypeStruct(out_shape, out_dtype),
        grid=(m // ROW_TILE, nb),
        in_specs=[pl.BlockSpec((ROW_TILE, ct), lambda i, j: (i, pb + j)),
                  pl.BlockSpec((SUBLANES, ct), lambda i, j: (jnp.maximum(i * rb - 1, 0), pb + j)),
                  pl.BlockSpec((SUBLANES, ct), lambda i, j: (jnp.minimum((i + 1) * rb, last), pb + j)),
                  pl.BlockSpec((3, ct), lambda i, j: (0, wb + j)),
                  pl.BlockSpec((1, ct), lambda i, j: (0, wb + j))],
        out_specs=out_spec,
        compiler_params=_cparams(("arbitrary", "arbitrary")),
        name="ssd_prep",
    )(p, p, p, conv_w, conv_b.reshape(1, -1))


def _softplus(x):
    return jnp.maximum(x, 0.0) + jnp.log1p(jnp.exp(-jnp.abs(x)))


def _ssd_scan_kernel(*refs, direction):
    if direction == 0:
        x_ref, b_ref, c_ref, dt_ref, dtt_ref, bias_r_ref, bias_c_ref, alog_r_ref, alog_c_ref, d_ref, \
            y_ref, h_ref, xw_ref = refs
        yf_ref = None
    else:
        x_ref, b_ref, c_ref, dt_ref, dtt_ref, bias_r_ref, bias_c_ref, alog_r_ref, alog_c_ref, yf_ref, \
            y_ref, h_ref, xw_ref = refs
        d_ref = None

    @pl.when(pl.program_id(1) == 0)
    def _():
        h_ref[...] = jnp.zeros(h_ref.shape, F32)

    dt = _softplus(dt_ref[...] + bias_r_ref[...])
    dtt = _softplus(dtt_ref[...] + bias_c_ref[...])
    dta = dt * (-jnp.exp(alog_r_ref[...]))
    dtat = dtt * (-jnp.exp(alog_c_ref[...]))
    row = lax.broadcasted_iota(jnp.int32, (CHUNK, CHUNK), 0)
    col = lax.broadcasted_iota(jnp.int32, (CHUNK, CHUNK), 1)
    mask = (row >= col) if direction == 0 else (row <= col)
    mask_t = (row <= col) if direction == 0 else (row >= col)
    tri = mask.astype(F32)
    a_all = jnp.dot(tri, dta, precision=HIGHEST, preferred_element_type=F32)
    at_all = lax.dot_general(dtat, tri, (((1,), (1,)), ((), ())), precision=HIGHEST,
                             preferred_element_type=F32)
    a_tot = jnp.sum(dtat, axis=1, keepdims=True)
    nt = (((1,), (1,)), ((), ()))

    for g in range(SSM_GROUPS):
        sl = slice((g % 4) * D_STATE, (g % 4 + 1) * D_STATE)
        bg = b_ref[g // 4, :, sl]
        cg = c_ref[g // 4, :, sl]
        cb_t = lax.dot_general(bg, cg, nt, preferred_element_type=F32)
        y_off = lax.dot_general(h_ref[g].astype(BF16), cg, nt, preferred_element_type=F32)
        for k in range(HPG):
            hd = direction * SSM_GROUPS * HPG + g * HPG + k
            ch = slice(k * SSM_HEAD_DIM, (k + 1) * SSM_HEAD_DIM)
            a_col = a_all[:, hd:hd + 1]
            a_row = at_all[hd:hd + 1, :]
            a_last = a_tot[hd:hd + 1, :]
            decay_t = jnp.exp(jnp.where(mask_t, a_row - a_col, NEG_INF))
            xk = x_ref[g, ch, :]
            xdt = xk * dtt[hd:hd + 1, :]
            y = jnp.dot(xdt.astype(BF16), (cb_t * decay_t).astype(BF16), preferred_element_type=F32)
            y = y + y_off[ch, :] * jnp.exp(a_row)
            if direction == 0:
                y = y + d_ref[hd:hd + 1, :] * xk
            else:
                y = y + yf_ref[g, ch, :]
            y_ref[g, ch, :] = y
            xw_ref[ch, :] = (xdt * jnp.exp(a_last - a_row)).astype(BF16)
            h_ref[g, ch, :] = h_ref[g, ch, :] * jnp.exp(a_last)
        h_ref[g] = h_ref[g] + jnp.dot(xw_ref[...], bg, preferred_element_type=F32)


def _ssd_scan(x_t, bc, dt_raw, dtt_raw, dt_bias, a_log, extra, direction, batch):
    _, _, m = x_t.shape
    chunks = m // CHUNK // batch
    ctx_chunks = ROW_TILE // CHUNK

    def rb(bi, t):
        if direction == 0:
            c = t
        else:
            c = jnp.where(t < ctx_chunks, ctx_chunks - 1 - t, chunks - 1 + ctx_chunks - t)
        return bi * chunks + c

    nh = dt_raw.shape[1]
    rowv = lambda v: v.reshape(1, nh)
    colv = lambda v: v.reshape(nh, 1)
    xspec = pl.BlockSpec((SSM_GROUPS, GROUP_W, CHUNK), lambda bi, t: (0, 0, rb(bi, t)))
    const = lambda shape: pl.BlockSpec(shape, lambda bi, t: (0, 0))
    extra_spec = const((nh, 1)) if direction == 0 else xspec
    return pl.pallas_call(
        functools.partial(_ssd_scan_kernel, direction=direction),
        out_shape=jax.ShapeDtypeStruct(x_t.shape, F32),
        grid=(batch, chunks),
        in_specs=[xspec,
                  pl.BlockSpec((2, CHUNK, GROUP_W), lambda bi, t: (2 * direction, rb(bi, t), 0)),
                  pl.BlockSpec((2, CHUNK, GROUP_W), lambda bi, t: (2 * direction + 1, rb(bi, t), 0)),
                  pl.BlockSpec((CHUNK, nh), lambda bi, t: (rb(bi, t), 0)),
                  pl.BlockSpec((nh, CHUNK), lambda bi, t: (0, rb(bi, t))),
                  const((1, nh)), const((nh, 1)), const((1, nh)), const((nh, 1)), extra_spec],
        out_specs=xspec,
        scratch_shapes=[pltpu.VMEM((SSM_GROUPS, GROUP_W, D_STATE), F32),
                        pltpu.VMEM((GROUP_W, CHUNK), BF16)],
        compiler_params=_cparams(("arbitrary", "arbitrary")),
        name=f"ssd_scan{direction}",
    )(x_t, bc, bc, dt_raw, dtt_raw, rowv(dt_bias), colv(dt_bias), rowv(a_log), colv(a_log), extra)


def _ssd_finish_kernel(y_ref, z_ref, nw_ref, o_ref):
    for g in range(SSM_GROUPS):
        cols = slice(g * GROUP_W, (g + 1) * GROUP_W)
        z = z_ref[:, cols]
        y = y_ref[g].T * (z * jax.nn.sigmoid(z))
        y = y * lax.rsqrt(jnp.mean(y * y, axis=-1, keepdims=True) + EPS)
        o_ref[:, cols] = (y * nw_ref[:, cols]).astype(o_ref.dtype)


def _ssd_finish(y_t, p, norm_w):
    _, _, m = y_t.shape
    d_inner = SSM_GROUPS * GROUP_W
    return pl.pallas_call(
        _ssd_finish_kernel,
        out_shape=jax.ShapeDtypeStruct((m, d_inner), BF16),
        grid=(m // ROW_TILE,),
        in_specs=[pl.BlockSpec((SSM_GROUPS, GROUP_W, ROW_TILE), lambda i: (0, 0, i)),
                  pl.BlockSpec((ROW_TILE, d_inner), lambda i: (i, 0)),
                  pl.BlockSpec((1, d_inner), lambda i: (0, 0))],
        out_specs=pl.BlockSpec((ROW_TILE, d_inner), lambda i: (i, 0)),
        compiler_params=_cparams(("arbitrary",)),
        name="ssd_finish",
    )(y_t, p, norm_w.reshape(1, d_inner))


def _token_major_load(ref, idx, n_c, dtype):
    return jnp.concatenate([ref[idx + (slice(None), c, slice(None))].astype(dtype) for c in range(n_c)], axis=-1)


def _norm_router_kernel(h_ref, g_ref, m_ref, w_ref, b_ref, f_ref, o_ref, cnt_ref, carry_ref, whi_ref, wlo_ref,
                        *, shift_row, scale_row):
    @pl.when(pl.program_id(0) == 0)
    def _():
        carry_ref[...] = jnp.zeros(carry_ref.shape, F32)
        w = w_ref[...]
        w_hi = w.astype(BF16)
        whi_ref[...] = w_hi
        wlo_ref[...] = (w - w_hi.astype(F32)).astype(BF16)

    x = h_ref[...]
    y = x * lax.rsqrt(jnp.mean(x * x, axis=-1, keepdims=True) + EPS) * g_ref[...]
    f = y * (1.0 + m_ref[0, scale_row:scale_row + 1, :]) + m_ref[0, shift_row:shift_row + 1, :]
    f_ref[...] = f

    f_hi = f.astype(BF16)
    f_lo = (f - f_hi.astype(F32)).astype(BF16)
    logits = (jnp.dot(f_hi, whi_ref[...], preferred_element_type=F32)
              + jnp.dot(f_lo, whi_ref[...], preferred_element_type=F32)
              + jnp.dot(f_hi, wlo_ref[...], preferred_element_type=F32)) + b_ref[...]
    rows = logits.shape[0]
    lane = lax.broadcasted_iota(jnp.int32, logits.shape, 1)
    big = jnp.int32(1 << 20)

    def first_lane(cond):
        return jnp.min(jnp.where(cond, lane, big), axis=-1, keepdims=True)

    gmask = lane < N_GROUPS
    gl = jnp.where(gmask, logits, NEG_INF)
    ge = jnp.exp(gl - jnp.max(gl, axis=-1, keepdims=True))
    g_prob = ge / jnp.sum(ge, axis=-1, keepdims=True)
    g_p = jnp.max(g_prob, axis=-1, keepdims=True)
    g_sel = first_lane(jnp.logical_and(gmask, g_prob == g_p))
    lo = N_GROUPS + EXP_PER_GROUP * g_sel
    emask = jnp.logical_and(lane >= lo, lane < lo + EXP_PER_GROUP)
    el = jnp.where(emask, logits, NEG_INF)
    ee = jnp.exp(el - jnp.max(el, axis=-1, keepdims=True))
    e_prob = jnp.where(emask, ee / jnp.sum(ee, axis=-1, keepdims=True), -1.0)
    p1 = jnp.max(e_prob, axis=-1, keepdims=True)
    i1 = first_lane(e_prob == p1)
    e_rest = jnp.where(lane == i1, -1.0, e_prob)
    p2 = jnp.max(e_rest, axis=-1, keepdims=True)
    i2 = first_lane(e_rest == p2)
    denom = p1 + p2
    gate1 = g_p * p1 / denom
    gate2 = g_p * p2 / denom
    e1 = i1 - N_GROUPS
    e2 = i2 - N_GROUPS

    oh1 = lane == e1
    oh2 = lane == e2
    ohs = jnp.logical_or(oh1, oh2).astype(F32)
    r_i = lax.broadcasted_iota(jnp.int32, (rows, rows), 0)
    c_i = lax.broadcasted_iota(jnp.int32, (rows, rows), 1)
    before = (r_i > c_i).astype(F32).astype(BF16)
    seen = jnp.dot(before, ohs.astype(BF16), preferred_element_type=F32) + carry_ref[...]
    rank1 = jnp.sum(jnp.where(oh1, seen, 0.0), axis=-1, keepdims=True)
    rank2 = jnp.sum(jnp.where(oh2, seen, 0.0), axis=-1, keepdims=True)
    carry_ref[...] = carry_ref[...] + jnp.sum(ohs, axis=0, keepdims=True)

    slab = jnp.zeros(logits.shape, F32)
    for idx, val in enumerate((e1.astype(F32), e2.astype(F32), gate1, gate2, rank1, rank2)):
        slab = jnp.where(lane == idx, val, slab)
    o_ref[...] = slab
    cnt_ref[...] = jnp.broadcast_to(carry_ref[...], cnt_ref.shape)


def _norm_router(h, g, modt, shift_row, scale_row, w, b, tpb, batch):
    t, d = h.shape
    return pl.pallas_call(
        functools.partial(_norm_router_kernel, shift_row=shift_row, scale_row=scale_row),
        out_shape=(jax.ShapeDtypeStruct((t, d), F32),
                   jax.ShapeDtypeStruct((t, LANES), F32),
                   jax.ShapeDtypeStruct((SUBLANES, LANES), F32)),
        grid=(t // ROW_TILE,),
        in_specs=[pl.BlockSpec((ROW_TILE, d), lambda i: (i, 0)),
                  pl.BlockSpec((1, d), lambda i: (0, 0)),
                  pl.BlockSpec((1, SUBLANES, d), lambda i: (_cond_of_tile(i, tpb, batch), 0, 0)),
                  pl.BlockSpec((d, LANES), lambda i: (0, 0)),
                  pl.BlockSpec((1, LANES), lambda i: (0, 0))],
        out_specs=(pl.BlockSpec((ROW_TILE, d), lambda i: (i, 0)),
                   pl.BlockSpec((ROW_TILE, LANES), lambda i: (i, 0)),
                   pl.BlockSpec((SUBLANES, LANES), lambda i: (0, 0))),
        scratch_shapes=[pltpu.VMEM((1, LANES), F32), pltpu.VMEM((d, LANES), BF16), pltpu.VMEM((d, LANES), BF16)],
        compiler_params=_cparams(("arbitrary",)),
        name="norm_router",
    )(h, g.reshape(1, d), modt, w, b)


def _row_copy(src_hbm, dst_ref, sem, src_row, dst_row, n_c):
    return pltpu.make_async_copy(src_hbm.at[src_row, pl.ds(0, n_c)], dst_ref.at[dst_row, pl.ds(0, n_c)], sem)


def _gather_rows_loop(idx_ref, k, src_hbm, dst_ref, sem, n_rows, n_c):
    def body(r, carry):
        _row_copy(src_hbm, dst_ref, sem, idx_ref[0, k, r], r, n_c).start()
        return carry

    lax.fori_loop(0, n_rows, body, 0)


def _gather_rows_unrolled(idx_ref, k, src_hbm, dst_ref, sem, n_rows, n_c):
    for r in range(n_rows):
        _row_copy(src_hbm, dst_ref, sem, idx_ref[0, k, r], r, n_c).start()


def _wait_rows(dst_ref, sem, n_c):
    view = dst_ref.at[:, pl.ds(0, n_c)]
    pltpu.make_async_copy(view, view, sem).wait()


def _expert_kernel(blk_e_ref, cur_ref, nxt_ref, f_hbm, wg_ref, wu_ref, wd_ref, o_ref, xbuf, xb_ref, sems):
    i = pl.program_id(0)
    slot = i % 2
    n_c = xb_ref.shape[1] // LANES

    @pl.when(i == 0)
    def _():
        _gather_rows_loop(cur_ref, 0, f_hbm, xbuf.at[0], sems.at[0], MOE_BLOCK, n_c)

    _wait_rows(xbuf.at[slot], sems.at[slot], n_c)
    xb_ref[...] = _token_major_load(xbuf, (slot,), n_c, BF16)
    _gather_rows_unrolled(nxt_ref, 0, f_hbm, xbuf.at[1 - slot], sems.at[1 - slot], MOE_BLOCK, n_c)
    x = xb_ref[...]
    hg = jnp.dot(x, wg_ref[0, 0], preferred_element_type=F32)
    hu = jnp.dot(x, wu_ref[0, 0], preferred_element_type=F32)
    act = (hg * jax.nn.sigmoid(hg) * hu).astype(BF16)
    o_ref[...] = jnp.dot(act, wd_ref[0, 0], preferred_element_type=F32)

    @pl.when(i == pl.num_programs(0) - 1)
    def _():
        _wait_rows(xbuf.at[1 - slot], sems.at[1 - slot], n_c)


def _experts(f, slot_t, blk_e, wg, wu, wd, layer):
    t = f.shape[0]
    d = wg.shape[2]
    n_blocks = blk_e.shape[0]
    de = wg.shape[3]
    slots3 = slot_t.reshape(n_blocks, 1, MOE_BLOCK)
    grid_spec = pltpu.PrefetchScalarGridSpec(
        num_scalar_prefetch=1,
        grid=(n_blocks,),
        in_specs=[pl.BlockSpec((1, 1, MOE_BLOCK), lambda i, be: (i, 0, 0), memory_space=pltpu.SMEM),
                  pl.BlockSpec((1, 1, MOE_BLOCK), lambda i, be: (jnp.minimum(i + 1, n_blocks - 1), 0, 0),
                               memory_space=pltpu.SMEM),
                  pl.BlockSpec(memory_space=pl.ANY),
                  pl.BlockSpec((1, 1, d, de), lambda i, be: (layer, be[i], 0, 0)),
                  pl.BlockSpec((1, 1, d, de), lambda i, be: (layer, be[i], 0, 0)),
                  pl.BlockSpec((1, 1, de, d), lambda i, be: (layer, be[i], 0, 0))],
        out_specs=pl.BlockSpec((MOE_BLOCK, d), lambda i, be: (i, 0)),
        scratch_shapes=[pltpu.VMEM((2, MOE_BLOCK, TOK_PITCH, LANES), F32), pltpu.VMEM((MOE_BLOCK, d), BF16),
                        pltpu.SemaphoreType.DMA((2,))],
    )
    return pl.pallas_call(
        _expert_kernel,
        out_shape=jax.ShapeDtypeStruct((n_blocks * MOE_BLOCK, d), F32),
        grid_spec=grid_spec,
        compiler_params=_cparams(("arbitrary",)),
        name="experts",
    )(blk_e, slots3, slots3, f, wg, wu, wd)


def _combine_kernel(cur_ref, nxt_ref, y_hbm, h_ref, slab_ref, m_ref, o_ref, ybuf, sems, *, gate_row):
    i = pl.program_id(0)
    slot = i % 2

    def row(src_row, dst_row, s, k):
        return pltpu.make_async_copy(y_hbm.at[pl.ds(src_row, 1)], ybuf.at[s, k, pl.ds(dst_row, 1)], sems.at[s, k])

    def wait(s, k):
        pltpu.make_async_copy(ybuf.at[s, k], ybuf.at[s, k], sems.at[s, k]).wait()

    @pl.when(i == 0)
    def _():
        for k in range(2):
            def body(r, carry, k=k):
                row(cur_ref[0, k, r], r, 0, k).start()
                return carry
            lax.fori_loop(0, ROW_TILE, body, 0)

    for k in range(2):
        wait(slot, k)
    for k in range(2):
        for r in range(ROW_TILE):
            row(nxt_ref[0, k, r], r, 1 - slot, k).start()
    slab = slab_ref[...]
    y = slab[:, 2:3] * ybuf[slot, 0] + slab[:, 3:4] * ybuf[slot, 1]
    o_ref[...] = h_ref[...] + m_ref[0, gate_row:gate_row + 1, :] * y

    @pl.when(i == pl.num_programs(0) - 1)
    def _():
        for k in range(2):
            wait(1 - slot, k)


def _combine(y_sorted, dest3, h, slab, modt, gate_row, tpb, batch):
    t, d = h.shape
    nt = t // ROW_TILE
    return pl.pallas_call(
        functools.partial(_combine_kernel, gate_row=gate_row),
        out_shape=jax.ShapeDtypeStruct((t, d), F32),
        grid=(nt,),
        in_specs=[pl.BlockSpec((1, 2, ROW_TILE), lambda i: (i, 0, 0), memory_space=pltpu.SMEM),
                  pl.BlockSpec((1, 2, ROW_TILE), lambda i: (jnp.minimum(i + 1, nt - 1), 0, 0),
                               memory_space=pltpu.SMEM),
                  pl.BlockSpec(memory_space=pl.ANY),
                  pl.BlockSpec((ROW_TILE, d), lambda i: (i, 0)),
                  pl.BlockSpec((ROW_TILE, LANES), lambda i: (i, 0)),
                  pl.BlockSpec((1, SUBLANES, d), lambda i: (_cond_of_tile(i, tpb, batch), 0, 0))],
        out_specs=pl.BlockSpec((ROW_TILE, d), lambda i: (i, 0)),
        scratch_shapes=[pltpu.VMEM((2, 2, ROW_TILE, d), F32), pltpu.SemaphoreType.DMA((2, 2))],
        compiler_params=_cparams(("arbitrary",)),
        name="moe_combine",
    )(dest3, dest3, y_sorted, h, slab, modt)


def _moe(h, g, modt, w_grp, b_grp, w_exp, b_exp, wg, wu, wd, layer, tpb, batch):
    t, d = h.shape
    n_r = N_GROUPS + N_EXPERTS
    w_r = jnp.zeros((d, LANES), F32).at[:, :n_r].set(jnp.concatenate([w_grp, w_exp], axis=1))
    b_r = jnp.zeros((1, LANES), F32).at[0, :n_r].set(jnp.concatenate([b_grp, b_exp]))
    f, slab, cnt = _norm_router(h, g, modt, 3, 4, w_r, b_r, tpb, batch)

    counts = cnt[0, :N_EXPERTS].astype(jnp.int32)
    pcounts = (counts + MOE_BLOCK - 1) // MOE_BLOCK * MOE_BLOCK
    pends = jnp.cumsum(pcounts)
    pstarts = pends - pcounts
    eid = slab[:, 0:2].astype(jnp.int32)
    rank = slab[:, 4:6].astype(jnp.int32)
    dest = pstarts[eid] + rank
    n_blocks = -(-2 * t // MOE_BLOCK) + N_EXPERTS
    n_slots = n_blocks * MOE_BLOCK
    tok = jnp.broadcast_to(jnp.arange(t, dtype=jnp.int32)[:, None], (t, 2))
    slot_t = jnp.zeros((n_slots,), jnp.int32).at[dest.reshape(-1)].set(tok.reshape(-1))
    blk_start = jnp.arange(n_blocks, dtype=jnp.int32) * MOE_BLOCK
    blk_e = jnp.minimum(jnp.sum(pends[None, :] <= blk_start[:, None], axis=1), N_EXPERTS - 1).astype(jnp.int32)

    y_sorted = _experts(f.reshape(t, d // LANES, LANES), slot_t, blk_e, wg, wu, wd, layer)
    dest3 = dest.reshape(t // ROW_TILE, ROW_TILE, 2).transpose(0, 2, 1)
    return _combine(y_sorted, dest3, h, slab, modt, 5, tpb, batch)


def _rope_tables(ctx_len, seq):
    n_rows = seq // GRID_W
    rowp = jnp.repeat(jnp.arange(n_rows, dtype=F32), GRID_W)
    colp = jnp.tile(jnp.arange(GRID_W, dtype=F32), n_rows)
    axis_dim = HEAD_DIM // 2
    inv = ROPE_THETA ** (-jnp.arange(0, axis_dim, 2, dtype=F32) / axis_dim)
    ar = rowp[:, None] * inv
    ac = colp[:, None] * inv
    cos = jnp.concatenate([jnp.cos(ar), jnp.cos(ar), jnp.cos(ac), jnp.cos(ac)], axis=-1)
    sin = jnp.concatenate([-jnp.sin(ar), jnp.sin(ar), -jnp.sin(ac), jnp.sin(ac)], axis=-1)
    cos = jnp.concatenate([jnp.ones((ctx_len, HEAD_DIM), F32), cos], axis=0)
    sin = jnp.concatenate([jnp.zeros((ctx_len, HEAD_DIM), F32), sin], axis=0)
    return cos, sin


def _pick_tile(n, candidates):
    for c in candidates:
        if n % c == 0:
            return c
    raise ValueError(f"no tile for {n}")


def kernel(x, c, ctx, c_ctx, ada_w, ada_b, norm_mix_g, norm_ffn_g, att_w_in, att_w_out, att_q_g, att_k_g, sconv_w, ssm_w_in, ssm_conv_w, ssm_conv_b, ssm_dt_bias, ssm_a_log, ssm_d, ssm_norm_w, ssm_w_out, router_grp_w, router_grp_b, router_exp_w, router_exp_b, exp_w_gate, exp_w_up, exp_w_down):
    b, s, d = x.shape
    cl = ctx.shape[1]
    assert cl == ROW_TILE and s % ROW_TILE == 0 and s % GRID_W == 0
    depth = ada_w.shape[0]
    l = cl + s
    m = b * l
    tpb = l // ROW_TILE
    d_inner = SSM_GROUPS * GROUP_W
    conv_dim = ssm_conv_w.shape[2]

    cond = jnp.concatenate([c, c_ctx[None]], axis=0)
    cond_x = jnp.broadcast_to(cond[:, :, None], (b + 1, d, LANES))
    mods = _ada_all(cond_x, ada_w, ada_b).reshape(depth, SUBLANES, 6, d)
    modt = jnp.pad(mods, ((0, 0), (0, 0), (0, SUBLANES - 6), (0, 0)))

    cos_t, sin_t = _rope_tables(cl, s)
    tm = _pick_tile(m, (512, 256))
    tk = _pick_tile(l, (2816, 1408, 768, 512, 256))

    wg, wu, wd = exp_w_gate.astype(BF16), exp_w_up.astype(BF16), exp_w_down.astype(BF16)
    w_att_out, w_ssm_out = att_w_out.astype(BF16), ssm_w_out.astype(BF16)
    h = jnp.concatenate([ctx, x], axis=1).reshape(m, d)
    for i in range(depth):
        j = i // 2
        n = _norm_mod(h, norm_mix_g[i], modt[i], 0, 1, BF16, tpb, b)
        if i % 2 == 0:
            p = _mm(n, att_w_in, j, 0, att_w_in.shape[2], F32, tm, 1536)
            q, k, v = _qkv_prep(p, cos_t, sin_t, att_q_g[j], att_k_g[j], tpb)
            o = _flash(q.reshape(b, l, ATT_WIDTH), k.reshape(b, l, KV_WIDTH), v.reshape(b, l, 2 * KV_WIDTH), cl, tk)
            cv = _sconv(p, sconv_w[j], tpb)
            y = jnp.concatenate([o.reshape(m, ATT_WIDTH), cv], axis=-1)
            h = _mm_res(y, w_att_out, j, h, modt[i], 2, 1024, tpb, b)
        else:
            zx_w = d_inner + conv_dim
            p = _mm(n, ssm_w_in, j, 0, zx_w, F32, tm, 1536)
            dt_raw = _mm(n, ssm_w_in, j, zx_w, ssm_w_in.shape[2] - zx_w, F32, tm, LANES)
            x_t = _ssd_prep(p, ssm_conv_w[j], ssm_conv_b[j], d_inner, 0, d_inner, True, F32, tpb)
            bc = _ssd_prep(p, ssm_conv_w[j], ssm_conv_b[j], 2 * d_inner, d_inner, conv_dim - d_inner, False, BF16, tpb)
            dtt_raw = dt_raw.T
            bias = ssm_dt_bias[j].reshape(-1)
            alog = ssm_a_log[j].reshape(-1)
            d_col = jnp.tile(ssm_d[j], 2).reshape(-1, 1)
            yf = _ssd_scan(x_t, bc, dt_raw, dtt_raw, bias, alog, d_col, 0, b)
            y_t = _ssd_scan(x_t, bc, dt_raw, dtt_raw, bias, alog, yf, 1, b)
            y = _ssd_finish(y_t, p, ssm_norm_w[j])
            h = _mm_res(y, w_ssm_out, j, h, modt[i], 2, 1024, tpb, b)
        h = _moe(h, norm_ffn_g[i], modt[i], router_grp_w[i], router_grp_b[i], router_exp_w[i], router_exp_b[i],
                 wg, wu, wd, i, tpb, b)
    return h.reshape(b, l, d)[:, cl:, :]
```

```python
import functools

import jax
import jax.numpy as jnp
from jax import lax
from jax.experimental import pallas as pl
from jax.experimental.pallas import tpu as pltpu

F32 = jnp.float32
BF16 = jnp.bfloat16
HIGHEST = lax.Precision.HIGHEST

EPS = 1e-6
LANES = 128
SUBLANES = 8
ROW_TILE = 256
GRID_W = 64
ROPE_THETA = 10000.0

ATT_HEADS = 8
ATT_KV_HEADS = 2
ATT_GROUP = ATT_HEADS // ATT_KV_HEADS
HEAD_DIM = 128
ATT_WIDTH = ATT_HEADS * HEAD_DIM
KV_WIDTH = ATT_KV_HEADS * HEAD_DIM
CONV_TILE = 512
PREP_TILE = 2048

SSM_HEAD_DIM = 64
SSM_GROUPS = 8
HPG = 8
D_STATE = 128
CHUNK = 128
GROUP_W = HPG * SSM_HEAD_DIM

N_GROUPS = 4
EXP_PER_GROUP = 8
N_EXPERTS = N_GROUPS * EXP_PER_GROUP
MOE_BLOCK = 256

NEG_INF = float("-inf")
LOG2_E = 1.4426950408889634


def _cparams(sem, vmem_mb=None):
    kw = dict(dimension_semantics=sem)
    if vmem_mb is not None:
        kw["vmem_limit_bytes"] = vmem_mb * 1024 * 1024
    return pltpu.CompilerParams(**kw)


def _lane_tile(x, n):
    return jnp.concatenate([x] * n, axis=-1)


def _ada_kernel(cx_ref, w_ref, b_ref, o_ref, s_ref, *, n_cond):
    @pl.when(jnp.logical_and(pl.program_id(0) == 0, pl.program_id(1) == 0))
    def _():
        cx = cx_ref[...]
        s_ref[...] = cx * jax.nn.sigmoid(cx)

    d = cx_ref.shape[1]
    tn = o_ref.shape[2]

    def body(i, accs):
        k0 = pl.multiple_of(i * SUBLANES, SUBLANES)
        w = w_ref[0, pl.ds(k0, SUBLANES), :]
        return tuple(acc + w * _lane_tile(s_ref[r, pl.ds(k0, SUBLANES), :], tn // LANES)
                     for r, acc in enumerate(accs))

    accs = lax.fori_loop(0, d // SUBLANES, body,
                         tuple(jnp.zeros((SUBLANES, tn), F32) for _ in range(n_cond)), unroll=4)
    rows = [jnp.sum(acc, axis=0, keepdims=True) + b_ref[0] for acc in accs]
    rows += [jnp.zeros((1, tn), F32)] * (SUBLANES - n_cond)
    o_ref[0] = jnp.concatenate(rows, axis=0)


def _ada_all(cond_x, ada_w, ada_b):
    depth, d, n = ada_w.shape
    n_cond = cond_x.shape[0]
    tn = 1024
    return pl.pallas_call(
        functools.partial(_ada_kernel, n_cond=n_cond),
        out_shape=jax.ShapeDtypeStruct((depth, SUBLANES, n), F32),
        grid=(depth, n // tn),
        in_specs=[pl.BlockSpec((n_cond, d, LANES), lambda l, j: (0, 0, 0)),
                  pl.BlockSpec((1, d, tn), lambda l, j: (l, 0, j)),
                  pl.BlockSpec((1, 1, tn), lambda l, j: (l, 0, j))],
        out_specs=pl.BlockSpec((1, SUBLANES, tn), lambda l, j: (l, 0, j)),
        scratch_shapes=[pltpu.VMEM((n_cond, d, LANES), F32)],
        compiler_params=_cparams(("arbitrary", "arbitrary")),
        name="ada_mod",
    )(cond_x, ada_w, ada_b.reshape(depth, 1, n))


def _norm_mod_kernel(h_ref, g_ref, m_ref, o_ref, *, shift_row, scale_row):
    x = h_ref[...]
    y = x * lax.rsqrt(jnp.mean(x * x, axis=-1, keepdims=True) + EPS) * g_ref[...]
    sh = m_ref[0, shift_row:shift_row + 1, :]
    sc = m_ref[0, scale_row:scale_row + 1, :]
    o_ref[...] = (y * (1.0 + sc) + sh).astype(o_ref.dtype)


def _cond_of_tile(i, tiles_per_batch, batch):
    return jnp.where(i % tiles_per_batch == 0, batch, i // tiles_per_batch)


def _norm_mod(h, g, modt, shift_row, scale_row, out_dtype, tpb, batch):
    m, d = h.shape
    return pl.pallas_call(
        functools.partial(_norm_mod_kernel, shift_row=shift_row, scale_row=scale_row),
        out_shape=jax.ShapeDtypeStruct((m, d), out_dtype),
        grid=(m // ROW_TILE,),
        in_specs=[pl.BlockSpec((ROW_TILE, d), lambda i: (i, 0)),
                  pl.BlockSpec((1, d), lambda i: (0, 0)),
                  pl.BlockSpec((1, SUBLANES, d), lambda i: (_cond_of_tile(i, tpb, batch), 0, 0))],
        out_specs=pl.BlockSpec((ROW_TILE, d), lambda i: (i, 0)),
        compiler_params=_cparams(("arbitrary",)),
        name="norm_mod",
    )(h, g.reshape(1, d), modt)


def _mm_kernel(x_ref, w_ref, o_ref, wb_ref):
    @pl.when(pl.program_id(1) == 0)
    def _():
        wb_ref[...] = w_ref[0].astype(BF16)

    o_ref[...] = jnp.dot(x_ref[...], wb_ref[...], preferred_element_type=F32).astype(o_ref.dtype)


def _mm(x, w3, layer, col0, n, out_dtype, tm, tn):
    m, k = x.shape
    cb = col0 // tn
    return pl.pallas_call(
        _mm_kernel,
        out_shape=jax.ShapeDtypeStruct((m, n), out_dtype),
        grid=(n // tn, m // tm),
        in_specs=[pl.BlockSpec((tm, k), lambda j, i: (i, 0)),
                  pl.BlockSpec((1, k, tn), lambda j, i: (layer, 0, cb + j))],
        out_specs=pl.BlockSpec((tm, tn), lambda j, i: (i, j)),
        scratch_shapes=[pltpu.VMEM((k, tn), BF16)],
        compiler_params=_cparams(("arbitrary", "arbitrary")),
        name="mm",
    )(x, w3)


def _mm_res_kernel(x_ref, w_ref, r_ref, m_ref, o_ref, *, gate_row):
    acc = jnp.dot(x_ref[...], w_ref[0], preferred_element_type=F32)
    o_ref[...] = r_ref[...] + m_ref[0, gate_row:gate_row + 1, :] * acc


def _mm_res(x, w3, layer, res, modt, gate_row, tn, tpb, batch):
    m, k = x.shape
    n = w3.shape[2]
    return pl.pallas_call(
        functools.partial(_mm_res_kernel, gate_row=gate_row),
        out_shape=jax.ShapeDtypeStruct((m, n), F32),
        grid=(n // tn, m // ROW_TILE),
        in_specs=[pl.BlockSpec((ROW_TILE, k), lambda j, i: (i, 0)),
                  pl.BlockSpec((1, k, tn), lambda j, i: (layer, 0, j)),
                  pl.BlockSpec((ROW_TILE, tn), lambda j, i: (i, j)),
                  pl.BlockSpec((1, SUBLANES, tn), lambda j, i: (_cond_of_tile(i, tpb, batch), 0, j))],
        out_specs=pl.BlockSpec((ROW_TILE, tn), lambda j, i: (i, j)),
        compiler_params=_cparams(("arbitrary", "arbitrary")),
        name="mm_res",
    )(x, w3, res, modt)


def _qkv_prep_kernel(p_ref, cos_ref, sin_ref, qg_ref, kg_ref, q_ref, k_ref, v_ref):
    cos = cos_ref[...]
    sin = sin_ref[...]
    lane = lax.broadcasted_iota(jnp.int32, cos.shape, 1)
    first_half = (lane % 64) < 32

    def norm_rope(x, g):
        y = x * lax.rsqrt(jnp.mean(x * x, axis=-1, keepdims=True) + EPS) * g
        partner = jnp.where(first_half, pltpu.roll(y, 96, axis=1), pltpu.roll(y, 32, axis=1))
        return y * cos + partner * sin

    scale = HEAD_DIM ** -0.5 * LOG2_E
    for h in range(ATT_HEADS):
        x = p_ref[:, h * HEAD_DIM:(h + 1) * HEAD_DIM]
        q_ref[:, h * HEAD_DIM:(h + 1) * HEAD_DIM] = (norm_rope(x, qg_ref[...]) * scale).astype(q_ref.dtype)
    ones = jnp.ones((p_ref.shape[0], HEAD_DIM), v_ref.dtype)
    for h in range(ATT_KV_HEADS):
        c0 = ATT_WIDTH + h * HEAD_DIM
        k_ref[:, h * HEAD_DIM:(h + 1) * HEAD_DIM] = norm_rope(p_ref[:, c0:c0 + HEAD_DIM], kg_ref[...]).astype(k_ref.dtype)
        v0 = ATT_WIDTH + KV_WIDTH + h * HEAD_DIM
        v_ref[:, 2 * h * HEAD_DIM:(2 * h + 1) * HEAD_DIM] = p_ref[:, v0:v0 + HEAD_DIM].astype(v_ref.dtype)
        v_ref[:, (2 * h + 1) * HEAD_DIM:(2 * h + 2) * HEAD_DIM] = ones


def _qkv_prep(p, cos_t, sin_t, q_g, k_g, tiles_per_batch):
    m = p.shape[0]
    w = ATT_WIDTH + 2 * KV_WIDTH
    return pl.pallas_call(
        _qkv_prep_kernel,
        out_shape=(jax.ShapeDtypeStruct((m, ATT_WIDTH), BF16),
                   jax.ShapeDtypeStruct((m, KV_WIDTH), BF16),
                   jax.ShapeDtypeStruct((m, 2 * KV_WIDTH), BF16)),
        grid=(m // ROW_TILE,),
        in_specs=[pl.BlockSpec((ROW_TILE, w), lambda i: (i, 0)),
                  pl.BlockSpec((ROW_TILE, HEAD_DIM), lambda i: (i % tiles_per_batch, 0)),
                  pl.BlockSpec((ROW_TILE, HEAD_DIM), lambda i: (i % tiles_per_batch, 0)),
                  pl.BlockSpec((1, HEAD_DIM), lambda i: (0, 0)),
                  pl.BlockSpec((1, HEAD_DIM), lambda i: (0, 0))],
        out_specs=(pl.BlockSpec((ROW_TILE, ATT_WIDTH), lambda i: (i, 0)),
                   pl.BlockSpec((ROW_TILE, KV_WIDTH), lambda i: (i, 0)),
                   pl.BlockSpec((ROW_TILE, 2 * KV_WIDTH), lambda i: (i, 0))),
        compiler_params=_cparams(("arbitrary",)),
        name="qkv_prep",
    )(p, cos_t, sin_t, q_g.reshape(1, HEAD_DIM), k_g.reshape(1, HEAD_DIM))


def _flash_kernel(q_ref, k_ref, v_ref, o_ref, m_ref, acc_ref, *, ctx_len, tk):
    qi = pl.program_id(2)
    n_keys = k_ref.shape[1]

    m_ref[...] = jnp.full(m_ref.shape, NEG_INF, F32)
    acc_ref[...] = jnp.zeros(acc_ref.shape, F32)

    def block(k, v):
        width = k.shape[0]
        for g in range(ATT_GROUP):
            q = q_ref[0, :, g * HEAD_DIM:(g + 1) * HEAD_DIM]
            s = lax.dot_general(q, k, (((1,), (1,)), ((), ())), preferred_element_type=F32)
            m_prev = m_ref[g]
            m_new = jnp.maximum(m_prev, jnp.max(s, axis=-1, keepdims=True))
            alpha = jnp.exp2(m_prev - m_new)
            p = jnp.exp2(s - _lane_tile(m_new, width // LANES))
            pv = jnp.dot(p.astype(BF16), v, preferred_element_type=F32)
            acc_ref[g] = _lane_tile(alpha, 2) * acc_ref[g] + pv
            m_ref[g] = m_new

    @pl.when(qi == 0)
    def _():
        block(k_ref[0, :ctx_len, :], v_ref[0, :ctx_len, :])

    @pl.when(qi > 0)
    def _():
        def body(j, carry):
            r0 = pl.multiple_of(j * tk, tk)
            block(k_ref[0, pl.ds(r0, tk), :], v_ref[0, pl.ds(r0, tk), :])
            return carry
        lax.fori_loop(0, n_keys // tk, body, 0)

    for g in range(ATT_GROUP):
        a = acc_ref[g]
        o_ref[0, :, g * HEAD_DIM:(g + 1) * HEAD_DIM] = (a[:, :HEAD_DIM] / a[:, HEAD_DIM:]).astype(o_ref.dtype)


def _flash(q, k, v, ctx_len, tk):
    b, l, _ = q.shape
    tq = ROW_TILE
    assert ctx_len == tq and l % tk == 0
    gw = ATT_GROUP * HEAD_DIM
    return pl.pallas_call(
        functools.partial(_flash_kernel, ctx_len=ctx_len, tk=tk),
        out_shape=jax.ShapeDtypeStruct((b, l, ATT_WIDTH), BF16),
        grid=(b, ATT_KV_HEADS, l // tq),
        in_specs=[pl.BlockSpec((1, tq, gw), lambda bi, h, qi: (bi, qi, h)),
                  pl.BlockSpec((1, l, HEAD_DIM), lambda bi, h, qi: (bi, 0, h)),
                  pl.BlockSpec((1, l, 2 * HEAD_DIM), lambda bi, h, qi: (bi, 0, h))],
        out_specs=pl.BlockSpec((1, tq, gw), lambda bi, h, qi: (bi, qi, h)),
        scratch_shapes=[pltpu.VMEM((ATT_GROUP, tq, LANES), F32),
                        pltpu.VMEM((ATT_GROUP, tq, 2 * HEAD_DIM), F32)],
        compiler_params=_cparams(("arbitrary", "arbitrary", "arbitrary")),
        name="flash",
    )(q, k, v)


def _conv3(u, prev_row, next_row, w_ref):
    rows = u.shape[0]
    row = lax.broadcasted_iota(jnp.int32, u.shape, 0)
    u_prev = jnp.where(row == 0, prev_row, pltpu.roll(u, 1, axis=0))
    u_next = jnp.where(row == rows - 1, next_row, pltpu.roll(u, rows - 1, axis=0))
    return u_prev * w_ref[0:1, :] + u * w_ref[1:2, :] + u_next * w_ref[2:3, :]


def _seq_edges(tiles_per_batch):
    tb = pl.program_id(0) % tiles_per_batch
    has_prev = tb >= 2
    has_next = jnp.logical_and(tb >= 1, tb < tiles_per_batch - 1)
    return has_prev, has_next


def _halo_specs(col_block, n_rows):
    rb = ROW_TILE // SUBLANES
    last = n_rows // SUBLANES - 1
    prev = pl.BlockSpec((SUBLANES, CONV_TILE), lambda i, j: (jnp.maximum(i * rb - 1, 0), col_block + j))
    nxt = pl.BlockSpec((SUBLANES, CONV_TILE), lambda i, j: (jnp.minimum((i + 1) * rb, last), col_block + j))
    return prev, nxt


def _sconv_kernel(gb_ref, gc_ref, gx_ref, gcp_ref, gxp_ref, gcn_ref, gxn_ref, w_ref, o_ref, *, tiles_per_batch):
    has_prev, has_next = _seq_edges(tiles_per_batch)
    u = gc_ref[...] * gx_ref[...]
    prev_row = jnp.where(has_prev, gcp_ref[SUBLANES - 1:SUBLANES, :] * gxp_ref[SUBLANES - 1:SUBLANES, :], 0.0)
    next_row = jnp.where(has_next, gcn_ref[0:1, :] * gxn_ref[0:1, :], 0.0)
    o_ref[...] = (gb_ref[...] * _conv3(u, prev_row, next_row, w_ref)).astype(o_ref.dtype)


def _sconv(p, conv_w, tiles_per_batch):
    m = p.shape[0]
    width = conv_w.shape[1]
    base = (ATT_WIDTH + 2 * KV_WIDTH) // CONV_TILE
    nb = width // CONV_TILE
    main = lambda off: pl.BlockSpec((ROW_TILE, CONV_TILE), lambda i, j: (i, off + j))
    gcp, gcn = _halo_specs(base + nb, m)
    gxp, gxn = _halo_specs(base + 2 * nb, m)
    return pl.pallas_call(
        functools.partial(_sconv_kernel, tiles_per_batch=tiles_per_batch),
        out_shape=jax.ShapeDtypeStruct((m, width), BF16),
        grid=(m // ROW_TILE, nb),
        in_specs=[main(base), main(base + nb), main(base + 2 * nb), gcp, gxp, gcn, gxn,
                  pl.BlockSpec((3, CONV_TILE), lambda i, j: (0, j))],
        out_specs=pl.BlockSpec((ROW_TILE, CONV_TILE), lambda i, j: (i, j)),
        compiler_params=_cparams(("arbitrary", "arbitrary")),
        name="sconv",
    )(p, p, p, p, p, p, p, conv_w)


def _ssd_prep_kernel(u_ref, up_ref, un_ref, w_ref, b_ref, o_ref, *, tiles_per_batch, transpose):
    has_prev, has_next = _seq_edges(tiles_per_batch)
    prev_row = jnp.where(has_prev, up_ref[SUBLANES - 1:SUBLANES, :], 0.0)
    next_row = jnp.where(has_next, un_ref[0:1, :], 0.0)
    y = _conv3(u_ref[...], prev_row, next_row, w_ref) + b_ref[...]
    y = y * jax.nn.sigmoid(y)
    for q in range(o_ref.shape[0]):
        yq = y[:, q * GROUP_W:(q + 1) * GROUP_W]
        o_ref[q] = (yq.T if transpose else yq).astype(o_ref.dtype)


def _ssd_prep(p, conv_w, conv_b, p_col, w_col, width, transpose, out_dtype, tiles_per_batch):
    m = p.shape[0]
    ct = PREP_TILE
    per = ct // GROUP_W
    pb, wb, nb = p_col // ct, w_col // ct, width // ct
    rb = ROW_TILE // SUBLANES
    last = m // SUBLANES - 1
    if transpose:
        out_shape = (width // GROUP_W, GROUP_W, m)
        out_spec = pl.BlockSpec((per, GROUP_W, ROW_TILE), lambda i, j: (j, 0, i))
    else:
        out_shape = (width // GROUP_W, m, GROUP_W)
        out_spec = pl.BlockSpec((per, ROW_TILE, GROUP_W), lambda i, j: (j, i, 0))
    return pl.pallas_call(
        functools.partial(_ssd_prep_kernel, tiles_per_batch=tiles_per_batch, transpose=transpose),
        out_shape=jax.ShapeDtypeStruct(out_shape, out_dtype),
        grid=(m // ROW_TILE, nb),
        in_specs=[pl.BlockSpec((ROW_TILE, ct), lambda i, j: (i, pb + j)),
                  pl.BlockSpec((SUBLANES, ct), lambda i, j: (jnp.maximum(i * rb - 1, 0), pb + j)),
                  pl.BlockSpec((SUBLANES, ct), lambda i, j: (jnp.minimum((i + 1) * rb, last), pb + j)),
                  pl.BlockSpec((3, ct), lambda i, j: (0, wb + j)),
                  pl.BlockSpec((1, ct), lambda i, j: (0, wb + j))],
        out_specs=out_spec,
        compiler_params=_cparams(("arbitrary", "arbitrary")),
        name="ssd_prep",
    )(p, p, p, conv_w, conv_b.reshape(1, -1))


def _softplus(x):
    return jnp.maximum(x, 0.0) + jnp.log1p(jnp.exp(-jnp.abs(x)))


def _ssd_scan_kernel(*refs, direction):
    if direction == 0:
        x_ref, b_ref, c_ref, dt_ref, dtt_ref, bias_r_ref, bias_c_ref, alog_r_ref, alog_c_ref, d_ref, \
            y_ref, h_ref, xw_ref = refs
        yf_ref = None
    else:
        x_ref, b_ref, c_ref, dt_ref, dtt_ref, bias_r_ref, bias_c_ref, alog_r_ref, alog_c_ref, yf_ref, \
            y_ref, h_ref, xw_ref = refs
        d_ref = None

    @pl.when(pl.program_id(1) == 0)
    def _():
        h_ref[...] = jnp.zeros(h_ref.shape, F32)

    dt = _softplus(dt_ref[...] + bias_r_ref[...])
    dtt = _softplus(dtt_ref[...] + bias_c_ref[...])
    dta = dt * (-jnp.exp(alog_r_ref[...]))
    dtat = dtt * (-jnp.exp(alog_c_ref[...]))
    row = lax.broadcasted_iota(jnp.int32, (CHUNK, CHUNK), 0)
    col = lax.broadcasted_iota(jnp.int32, (CHUNK, CHUNK), 1)
    mask = (row >= col) if direction == 0 else (row <= col)
    mask_t = (row <= col) if direction == 0 else (row >= col)
    tri = mask.astype(F32)
    a_all = jnp.dot(tri, dta, precision=HIGHEST, preferred_element_type=F32)
    at_all = lax.dot_general(dtat, tri, (((1,), (1,)), ((), ())), precision=HIGHEST,
                             preferred_element_type=F32)
    a_tot = jnp.sum(dtat, axis=1, keepdims=True)
    nt = (((1,), (1,)), ((), ()))

    for g in range(SSM_GROUPS):
        sl = slice((g % 4) * D_STATE, (g % 4 + 1) * D_STATE)
        bg = b_ref[g // 4, :, sl]
        cg = c_ref[g // 4, :, sl]
        cb_t = lax.dot_general(bg, cg, nt, preferred_element_type=F32)
        y_off = lax.dot_general(h_ref[g].astype(BF16), cg, nt, preferred_element_type=F32)
        for k in range(HPG):
            hd = direction * SSM_GROUPS * HPG + g * HPG + k
            ch = slice(k * SSM_HEAD_DIM, (k + 1) * SSM_HEAD_DIM)
            a_col = a_all[:, hd:hd + 1]
            a_row = at_all[hd:hd + 1, :]
            a_last = a_tot[hd:hd + 1, :]
            decay_t = jnp.exp(jnp.where(mask_t, a_row - a_col, NEG_INF))
            xk = x_ref[g, ch, :]
            xdt = xk * dtt[hd:hd + 1, :]
            y = jnp.dot(xdt.astype(BF16), (cb_t * decay_t).astype(BF16), preferred_element_type=F32)
            y = y + y_off[ch, :] * jnp.exp(a_row)
            if direction == 0:
                y = y + d_ref[hd:hd + 1, :] * xk
            else:
                y = y + yf_ref[g, ch, :]
            y_ref[g, ch, :] = y
            xw_ref[ch, :] = (xdt * jnp.exp(a_last - a_row)).astype(BF16)
            h_ref[g, ch, :] = h_ref[g, ch, :] * jnp.exp(a_last)
        h_ref[g] = h_ref[g] + jnp.dot(xw_ref[...], bg, preferred_element_type=F32)


def _ssd_scan(x_t, bc, dt_raw, dtt_raw, dt_bias, a_log, extra, direction, batch):
    _, _, m = x_t.shape
    chunks = m // CHUNK // batch
    ctx_chunks = ROW_TILE // CHUNK

    def rb(bi, t):
        if direction == 0:
            c = t
        else:
            c = jnp.where(t < ctx_chunks, ctx_chunks - 1 - t, chunks - 1 + ctx_chunks - t)
        return bi * chunks + c

    nh = dt_raw.shape[1]
    rowv = lambda v: v.reshape(1, nh)
    colv = lambda v: v.reshape(nh, 1)
    xspec = pl.BlockSpec((SSM_GROUPS, GROUP_W, CHUNK), lambda bi, t: (0, 0, rb(bi, t)))
    const = lambda shape: pl.BlockSpec(shape, lambda bi, t: (0, 0))
    extra_spec = const((nh, 1)) if direction == 0 else xspec
    return pl.pallas_call(
        functools.partial(_ssd_scan_kernel, direction=direction),
        out_shape=jax.ShapeDtypeStruct(x_t.shape, F32),
        grid=(batch, chunks),
        in_specs=[xspec,
                  pl.BlockSpec((2, CHUNK, GROUP_W), lambda bi, t: (2 * direction, rb(bi, t), 0)),
                  pl.BlockSpec((2, CHUNK, GROUP_W), lambda bi, t: (2 * direction + 1, rb(bi, t), 0)),
                  pl.BlockSpec((CHUNK, nh), lambda bi, t: (rb(bi, t), 0)),
                  pl.BlockSpec((nh, CHUNK), lambda bi, t: (0, rb(bi, t))),
                  const((1, nh)), const((nh, 1)), const((1, nh)), const((nh, 1)), extra_spec],
        out_specs=xspec,
        scratch_shapes=[pltpu.VMEM((SSM_GROUPS, GROUP_W, D_STATE), F32),
                        pltpu.VMEM((GROUP_W, CHUNK), BF16)],
        compiler_params=_cparams(("arbitrary", "arbitrary")),
        name=f"ssd_scan{direction}",
    )(x_t, bc, bc, dt_raw, dtt_raw, rowv(dt_bias), colv(dt_bias), rowv(a_log), colv(a_log), extra)


def _ssd_finish_kernel(y_ref, z_ref, nw_ref, o_ref):
    for g in range(SSM_GROUPS):
        cols = slice(g * GROUP_W, (g + 1) * GROUP_W)
        z = z_ref[:, cols]
        y = y_ref[g].T * (z * jax.nn.sigmoid(z))
        y = y * lax.rsqrt(jnp.mean(y * y, axis=-1, keepdims=True) + EPS)
        o_ref[:, cols] = (y * nw_ref[:, cols]).astype(o_ref.dtype)


def _ssd_finish(y_t, p, norm_w):
    _, _, m = y_t.shape
    d_inner = SSM_GROUPS * GROUP_W
    return pl.pallas_call(
        _ssd_finish_kernel,
        out_shape=jax.ShapeDtypeStruct((m, d_inner), BF16),
        grid=(m // ROW_TILE,),
        in_specs=[pl.BlockSpec((SSM_GROUPS, GROUP_W, ROW_TILE), lambda i: (0, 0, i)),
                  pl.BlockSpec((ROW_TILE, d_inner), lambda i: (i, 0)),
                  pl.BlockSpec((1, d_inner), lambda i: (0, 0))],
        out_specs=pl.BlockSpec((ROW_TILE, d_inner), lambda i: (i, 0)),
        compiler_params=_cparams(("arbitrary",)),
        name="ssd_finish",
    )(y_t, p, norm_w.reshape(1, d_inner))


def _norm_router_kernel(h_ref, g_ref, m_ref, w_ref, b_ref, f_ref, o_ref, cnt_ref, carry_ref, whi_ref, wlo_ref,
                        *, shift_row, scale_row):
    @pl.when(pl.program_id(0) == 0)
    def _():
        carry_ref[...] = jnp.zeros(carry_ref.shape, F32)
        w = w_ref[...]
        w_hi = w.astype(BF16)
        whi_ref[...] = w_hi
        wlo_ref[...] = (w - w_hi.astype(F32)).astype(BF16)

    x = h_ref[...]
    y = x * lax.rsqrt(jnp.mean(x * x, axis=-1, keepdims=True) + EPS) * g_ref[...]
    f = y * (1.0 + m_ref[0, scale_row:scale_row + 1, :]) + m_ref[0, shift_row:shift_row + 1, :]
    f_ref[...] = f

    f_hi = f.astype(BF16)
    f_lo = (f - f_hi.astype(F32)).astype(BF16)
    logits = (jnp.dot(f_hi, whi_ref[...], preferred_element_type=F32)
              + jnp.dot(f_lo, whi_ref[...], preferred_element_type=F32)
              + jnp.dot(f_hi, wlo_ref[...], preferred_element_type=F32)) + b_ref[...]
    rows = logits.shape[0]
    lane = lax.broadcasted_iota(jnp.int32, logits.shape, 1)
    big = jnp.int32(1 << 20)

    def first_lane(cond):
        return jnp.min(jnp.where(cond, lane, big), axis=-1, keepdims=True)

    gmask = lane < N_GROUPS
    gl = jnp.where(gmask, logits, NEG_INF)
    ge = jnp.exp(gl - jnp.max(gl, axis=-1, keepdims=True))
    g_prob = ge / jnp.sum(ge, axis=-1, keepdims=True)
    g_p = jnp.max(g_prob, axis=-1, keepdims=True)
    g_sel = first_lane(jnp.logical_and(gmask, g_prob == g_p))
    lo = N_GROUPS + EXP_PER_GROUP * g_sel
    emask = jnp.logical_and(lane >= lo, lane < lo + EXP_PER_GROUP)
    el = jnp.where(emask, logits, NEG_INF)
    ee = jnp.exp(el - jnp.max(el, axis=-1, keepdims=True))
    e_prob = jnp.where(emask, ee / jnp.sum(ee, axis=-1, keepdims=True), -1.0)
    p1 = jnp.max(e_prob, axis=-1, keepdims=True)
    i1 = first_lane(e_prob == p1)
    e_rest = jnp.where(lane == i1, -1.0, e_prob)
    p2 = jnp.max(e_rest, axis=-1, keepdims=True)
    i2 = first_lane(e_rest == p2)
    denom = p1 + p2
    gate1 = g_p * p1 / denom
    gate2 = g_p * p2 / denom
    e1 = i1 - N_GROUPS
    e2 = i2 - N_GROUPS

    oh1 = lane == e1
    oh2 = lane == e2
    ohs = jnp.logical_or(oh1, oh2).astype(F32)
    r_i = lax.broadcasted_iota(jnp.int32, (rows, rows), 0)
    c_i = lax.broadcasted_iota(jnp.int32, (rows, rows), 1)
    before = (r_i > c_i).astype(F32).astype(BF16)
    seen = jnp.dot(before, ohs.astype(BF16), preferred_element_type=F32) + carry_ref[...]
    rank1 = jnp.sum(jnp.where(oh1, seen, 0.0), axis=-1, keepdims=True)
    rank2 = jnp.sum(jnp.where(oh2, seen, 0.0), axis=-1, keepdims=True)
    carry_ref[...] = carry_ref[...] + jnp.sum(ohs, axis=0, keepdims=True)

    slab = jnp.zeros(logits.shape, F32)
    for idx, val in enumerate((e1.astype(F32), e2.astype(F32), gate1, gate2, rank1, rank2)):
        slab = jnp.where(lane == idx, val, slab)
    o_ref[...] = slab
    cnt_ref[...] = jnp.broadcast_to(carry_ref[...], cnt_ref.shape)


def _norm_router(h, g, modt, shift_row, scale_row, w, b, tpb, batch):
    t, d = h.shape
    return pl.pallas_call(
        functools.partial(_norm_router_kernel, shift_row=shift_row, scale_row=scale_row),
        out_shape=(jax.ShapeDtypeStruct((t, d), F32),
                   jax.ShapeDtypeStruct((t, LANES), F32),
                   jax.ShapeDtypeStruct((SUBLANES, LANES), F32)),
        grid=(t // ROW_TILE,),
        in_specs=[pl.BlockSpec((ROW_TILE, d), lambda i: (i, 0)),
                  pl.BlockSpec((1, d), lambda i: (0, 0)),
                  pl.BlockSpec((1, SUBLANES, d), lambda i: (_cond_of_tile(i, tpb, batch), 0, 0)),
                  pl.BlockSpec((d, LANES), lambda i: (0, 0)),
                  pl.BlockSpec((1, LANES), lambda i: (0, 0))],
        out_specs=(pl.BlockSpec((ROW_TILE, d), lambda i: (i, 0)),
                   pl.BlockSpec((ROW_TILE, LANES), lambda i: (i, 0)),
                   pl.BlockSpec((SUBLANES, LANES), lambda i: (0, 0))),
        scratch_shapes=[pltpu.VMEM((1, LANES), F32), pltpu.VMEM((d, LANES), BF16), pltpu.VMEM((d, LANES), BF16)],
        compiler_params=_cparams(("arbitrary",)),
        name="norm_router",
    )(h, g.reshape(1, d), modt, w, b)


def _expert_kernel(blk_e_ref, nu_ref, cur_ref, nxt_ref, f_hbm, wg_ref, wu_ref, wd_ref, o_ref, xbuf, xb_ref, sems):
    i = pl.program_id(0)
    slot = i % 2
    blk = xbuf.shape[1]

    def row(token, r, s):
        return pltpu.make_async_copy(f_hbm.at[pl.ds(token, 1)], xbuf.at[s, pl.ds(r, 1)], sems.at[s])

    def wait(s):
        pltpu.make_async_copy(xbuf.at[s], xbuf.at[s], sems.at[s]).wait()

    @pl.when(i == 0)
    def _():
        def body(r, carry):
            row(cur_ref[0, 0, r], r, 0).start()
            return carry
        lax.fori_loop(0, blk, body, 0)

    for r in range(blk):
        row(nxt_ref[0, 0, r], r, 1 - slot).start()
    wait(slot)

    @pl.when(i < nu_ref[0])
    def _():
        xb_ref[...] = xbuf[slot].astype(BF16)
        x = xb_ref[...]
        hg = jnp.dot(x, wg_ref[0, 0], preferred_element_type=F32)
        hu = jnp.dot(x, wu_ref[0, 0], preferred_element_type=F32)
        act = (hg * jax.nn.sigmoid(hg) * hu).astype(BF16)
        o_ref[...] = jnp.dot(act, wd_ref[0, 0], preferred_element_type=F32)

    @pl.when(i >= nu_ref[0])
    def _():
        o_ref[...] = jnp.zeros(o_ref.shape, F32)

    @pl.when(i == pl.num_programs(0) - 1)
    def _():
        wait(1 - slot)


def _experts(f, slot_t, blk_e, n_used, wg, wu, wd, layer):
    t, d = f.shape
    n_blocks = blk_e.shape[0]
    de = wg.shape[3]
    blk = MOE_BLOCK
    slots3 = slot_t.reshape(n_blocks, 1, blk)
    grid_spec = pltpu.PrefetchScalarGridSpec(
        num_scalar_prefetch=2,
        grid=(n_blocks,),
        in_specs=[pl.BlockSpec((1, 1, blk), lambda i, be, nu: (i, 0, 0), memory_space=pltpu.SMEM),
                  pl.BlockSpec((1, 1, blk), lambda i, be, nu: (jnp.minimum(i + 1, n_blocks - 1), 0, 0),
                               memory_space=pltpu.SMEM),
                  pl.BlockSpec(memory_space=pl.ANY),
                  pl.BlockSpec((1, 1, d, de), lambda i, be, nu: (layer, be[i], 0, 0)),
                  pl.BlockSpec((1, 1, d, de), lambda i, be, nu: (layer, be[i], 0, 0)),
                  pl.BlockSpec((1, 1, de, d), lambda i, be, nu: (layer, be[i], 0, 0))],
        out_specs=pl.BlockSpec((blk, d), lambda i, be, nu: (i, 0)),
        scratch_shapes=[pltpu.VMEM((2, blk, d), F32), pltpu.VMEM((blk, d), BF16),
                        pltpu.SemaphoreType.DMA((2,))],
    )
    return pl.pallas_call(
        _expert_kernel,
        out_shape=jax.ShapeDtypeStruct((n_blocks * blk, d), F32),
        grid_spec=grid_spec,
        compiler_params=_cparams(("arbitrary",)),
        name="experts",
    )(blk_e, n_used, slots3, slots3, f, wg, wu, wd)


def _combine_kernel(cur_ref, nxt_ref, y_hbm, h_ref, slab_ref, m_ref, o_ref, ybuf, sems, *, gate_row):
    i = pl.program_id(0)
    slot = i % 2

    def row(src_row, dst_row, s, k):
        return pltpu.make_async_copy(y_hbm.at[pl.ds(src_row, 1)], ybuf.at[s, k, pl.ds(dst_row, 1)], sems.at[s, k])

    def wait(s, k):
        pltpu.make_async_copy(ybuf.at[s, k], ybuf.at[s, k], sems.at[s, k]).wait()

    @pl.when(i == 0)
    def _():
        for k in range(2):
            def body(r, carry, k=k):
                row(cur_ref[0, k, r], r, 0, k).start()
                return carry
            lax.fori_loop(0, ROW_TILE, body, 0)

    for k in range(2):
        wait(slot, k)
    for k in range(2):
        for r in range(ROW_TILE):
            row(nxt_ref[0, k, r], r, 1 - slot, k).start()
    slab = slab_ref[...]
    y = slab[:, 2:3] * ybuf[slot, 0] + slab[:, 3:4] * ybuf[slot, 1]
    o_ref[...] = h_ref[...] + m_ref[0, gate_row:gate_row + 1, :] * y

    @pl.when(i == pl.num_programs(0) - 1)
    def _():
        for k in range(2):
            wait(1 - slot, k)


def _combine(y_sorted, dest3, h, slab, modt, gate_row, tpb, batch):
    t, d = h.shape
    nt = t // ROW_TILE
    return pl.pallas_call(
        functools.partial(_combine_kernel, gate_row=gate_row),
        out_shape=jax.ShapeDtypeStruct((t, d), F32),
        grid=(nt,),
        in_specs=[pl.BlockSpec((1, 2, ROW_TILE), lambda i: (i, 0, 0), memory_space=pltpu.SMEM),
                  pl.BlockSpec((1, 2, ROW_TILE), lambda i: (jnp.minimum(i + 1, nt - 1), 0, 0),
                               memory_space=pltpu.SMEM),
                  pl.BlockSpec(memory_space=pl.ANY),
                  pl.BlockSpec((ROW_TILE, d), lambda i: (i, 0)),
                  pl.BlockSpec((ROW_TILE, LANES), lambda i: (i, 0)),
                  pl.BlockSpec((1, SUBLANES, d), lambda i: (_cond_of_tile(i, tpb, batch), 0, 0))],
        out_specs=pl.BlockSpec((ROW_TILE, d), lambda i: (i, 0)),
        scratch_shapes=[pltpu.VMEM((2, 2, ROW_TILE, d), F32), pltpu.SemaphoreType.DMA((2, 2))],
        compiler_params=_cparams(("arbitrary",)),
        name="moe_combine",
    )(dest3, dest3, y_sorted, h, slab, modt)


def _moe(h, g, modt, w_grp, b_grp, w_exp, b_exp, wg, wu, wd, layer, tpb, batch):
    t, d = h.shape
    n_r = N_GROUPS + N_EXPERTS
    w_r = jnp.zeros((d, LANES), F32).at[:, :n_r].set(jnp.concatenate([w_grp, w_exp], axis=1))
    b_r = jnp.zeros((1, LANES), F32).at[0, :n_r].set(jnp.concatenate([b_grp, b_exp]))
    f, slab, cnt = _norm_router(h, g, modt, 3, 4, w_r, b_r, tpb, batch)

    blk = MOE_BLOCK
    counts = cnt[0, :N_EXPERTS].astype(jnp.int32)
    pcounts = (counts + blk - 1) // blk * blk
    pends = jnp.cumsum(pcounts)
    pstarts = pends - pcounts
    eid = slab[:, 0:2].astype(jnp.int32)
    rank = slab[:, 4:6].astype(jnp.int32)
    dest = pstarts[eid] + rank
    n_blocks = -(-2 * t // blk) + N_EXPERTS
    n_slots = n_blocks * blk
    tok = jnp.broadcast_to(jnp.arange(t, dtype=jnp.int32)[:, None], (t, 2))
    slot_t = jnp.zeros((n_slots,), jnp.int32).at[dest.reshape(-1)].set(tok.reshape(-1))
    blk_start = jnp.arange(n_blocks, dtype=jnp.int32) * blk
    blk_e = jnp.minimum(jnp.sum(pends[None, :] <= blk_start[:, None], axis=1), N_EXPERTS - 1).astype(jnp.int32)
    n_used = (pends[-1:] // blk).astype(jnp.int32)

    y_sorted = _experts(f, slot_t, blk_e, n_used, wg, wu, wd, layer)
    dest3 = dest.reshape(t // ROW_TILE, ROW_TILE, 2).transpose(0, 2, 1)
    return _combine(y_sorted, dest3, h, slab, modt, 5, tpb, batch)


def _rope_tables(ctx_len, seq):
    n_rows = seq // GRID_W
    rowp = jnp.repeat(jnp.arange(n_rows, dtype=F32), GRID_W)
    colp = jnp.tile(jnp.arange(GRID_W, dtype=F32), n_rows)
    axis_dim = HEAD_DIM // 2
    inv = ROPE_THETA ** (-jnp.arange(0, axis_dim, 2, dtype=F32) / axis_dim)
    ar = rowp[:, None] * inv
    ac = colp[:, None] * inv
    cos = jnp.concatenate([jnp.cos(ar), jnp.cos(ar), jnp.cos(ac), jnp.cos(ac)], axis=-1)
    sin = jnp.concatenate([-jnp.sin(ar), jnp.sin(ar), -jnp.sin(ac), jnp.sin(ac)], axis=-1)
    cos = jnp.concatenate([jnp.ones((ctx_len, HEAD_DIM), F32), cos], axis=0)
    sin = jnp.concatenate([jnp.zeros((ctx_len, HEAD_DIM), F32), sin], axis=0)
    return cos, sin


def _pick_tile(n, candidates):
    for c in candidates:
        if n % c == 0:
            return c
    raise ValueError(f"no tile for {n}")


def kernel(x, c, ctx, c_ctx, ada_w, ada_b, norm_mix_g, norm_ffn_g, att_w_in, att_w_out, att_q_g, att_k_g, sconv_w, ssm_w_in, ssm_conv_w, ssm_conv_b, ssm_dt_bias, ssm_a_log, ssm_d, ssm_norm_w, ssm_w_out, router_grp_w, router_grp_b, router_exp_w, router_exp_b, exp_w_gate, exp_w_up, exp_w_down):
    b, s, d = x.shape
    cl = ctx.shape[1]
    assert cl == ROW_TILE and s % ROW_TILE == 0 and s % GRID_W == 0
    depth = ada_w.shape[0]
    l = cl + s
    m = b * l
    tpb = l // ROW_TILE
    d_inner = SSM_GROUPS * GROUP_W
    conv_dim = ssm_conv_w.shape[2]

    cond = jnp.concatenate([c, c_ctx[None]], axis=0)
    cond_x = jnp.broadcast_to(cond[:, :, None], (b + 1, d, LANES))
    mods = _ada_all(cond_x, ada_w, ada_b).reshape(depth, SUBLANES, 6, d)
    modt = jnp.pad(mods, ((0, 0), (0, 0), (0, SUBLANES - 6), (0, 0)))

    cos_t, sin_t = _rope_tables(cl, s)
    tm = _pick_tile(m, (512, 256))
    tk = _pick_tile(l, (2816, 1408, 768, 512, 256))

    wg, wu, wd = exp_w_gate.astype(BF16), exp_w_up.astype(BF16), exp_w_down.astype(BF16)
    w_att_out, w_ssm_out = att_w_out.astype(BF16), ssm_w_out.astype(BF16)
    h = jnp.concatenate([ctx, x], axis=1).reshape(m, d)
    for i in range(depth):
        j = i // 2
        n = _norm_mod(h, norm_mix_g[i], modt[i], 0, 1, BF16, tpb, b)
        if i % 2 == 0:
            p = _mm(n, att_w_in, j, 0, att_w_in.shape[2], F32, tm, 1536)
            q, k, v = _qkv_prep(p, cos_t, sin_t, att_q_g[j], att_k_g[j], tpb)
            o = _flash(q.reshape(b, l, ATT_WIDTH), k.reshape(b, l, KV_WIDTH), v.reshape(b, l, 2 * KV_WIDTH), cl, tk)
            cv = _sconv(p, sconv_w[j], tpb)
            y = jnp.concatenate([o.reshape(m, ATT_WIDTH), cv], axis=-1)
            h = _mm_res(y, w_att_out, j, h, modt[i], 2, 2048, tpb, b)
        else:
            zx_w = d_inner + conv_dim
            p = _mm(n, ssm_w_in, j, 0, zx_w, F32, tm, 1536)
            dt_raw = _mm(n, ssm_w_in, j, zx_w, ssm_w_in.shape[2] - zx_w, F32, tm, LANES)
            x_t = _ssd_prep(p, ssm_conv_w[j], ssm_conv_b[j], d_inner, 0, d_inner, True, F32, tpb)
            bc = _ssd_prep(p, ssm_conv_w[j], ssm_conv_b[j], 2 * d_inner, d_inner, conv_dim - d_inner, False, BF16, tpb)
            dtt_raw = dt_raw.T
            bias = ssm_dt_bias[j].reshape(-1)
            alog = ssm_a_log[j].reshape(-1)
            d_col = jnp.tile(ssm_d[j], 2).reshape(-1, 1)
            yf = _ssd_scan(x_t, bc, dt_raw, dtt_raw, bias, alog, d_col, 0, b)
            y_t = _ssd_scan(x_t, bc, dt_raw, dtt_raw, bias, alog, yf, 1, b)
            y = _ssd_finish(y_t, p, ssm_norm_w[j])
            h = _mm_res(y, w_ssm_out, j, h, modt[i], 2, 1024, tpb, b)
        h = _moe(h, norm_ffn_g[i], modt[i], router_grp_w[i], router_grp_b[i], router_exp_w[i], router_exp_b[i],
                 wg, wu, wd, i, tpb, b)
    return h.reshape(b, l, d)[:, cl:, :]
```
